```python
import math
import jax
import jax.numpy as jnp
from jax import lax
import numpy as np

D_MODEL = 2048
BATCH = 2
SEQ = 8192
DEPTH = 4

GRID_W = 64
CTX_LEN = 256
N_MIXERS = 3
N_A = (DEPTH + 2) // 3
N_B = (DEPTH + 1) // 3
N_C = DEPTH // 3
D_FF = 4 * D_MODEL
N_MOD = 6
EPS = 1e-6
ROPE_BASE = 10000.0
Q_BLOCK = 128

CHUNK = 128
GMLP_WIDTH = D_MODEL
GMLP_GROUPS = 16
GMLP_GROUP_DIM = GMLP_WIDTH // GMLP_GROUPS

DIFF_HEADS = D_MODEL // 256
DIFF_HEAD_DIM = 128
DIFF_V_DIM = 2 * DIFF_HEAD_DIM

MLA_HEADS = D_MODEL // 128
MLA_Q_RANK = 448
MLA_KV_RANK = 512
MLA_NOPE = 128
MLA_ROPE = 64
MLA_V = 128
MLA_QK = MLA_NOPE + MLA_ROPE

kernel_name = "hybrid_gmlp_diffattn_mla_dit"


def rms_norm(x, g):
    xf = x.astype(jnp.float32)
    y = xf * lax.rsqrt(jnp.mean(xf * xf, axis=-1, keepdims=True) + EPS)
    return (y * g.astype(jnp.float32)).astype(x.dtype)


def layer_norm(x, g):
    xf = x.astype(jnp.float32)
    xc = xf - jnp.mean(xf, axis=-1, keepdims=True)
    y = xc * lax.rsqrt(jnp.mean(xc * xc, axis=-1, keepdims=True) + EPS)
    return (y * g.astype(jnp.float32)).astype(x.dtype)


def modulate(h, shift, scale):
    return h * (1 + scale) + shift


def axial_rope_tables(rows, cols, rot_dim):
    a = rot_dim // 2
    inv = ROPE_BASE ** (-jnp.arange(0, a, 2, dtype=jnp.float32) / a)

    def axis_angles(pos):
        ang = pos.astype(jnp.float32)[:, None] * inv[None, :]
        return jnp.concatenate([ang, ang], axis=-1)

    ang = jnp.concatenate([axis_angles(rows), axis_angles(cols)], axis=-1)
    return jnp.cos(ang), jnp.sin(ang)


def _rotate_half(t):
    half = t.shape[-1] // 2
    return jnp.concatenate([-t[..., half:], t[..., :half]], axis=-1)


def apply_axial_rope(x, cos, sin):
    a = x.shape[-1] // 2
    xr = jnp.concatenate([_rotate_half(x[..., :a]), _rotate_half(x[..., a:])], axis=-1)
    bshape = (1, cos.shape[0]) + (1,) * (x.ndim - 3) + (cos.shape[-1],)
    return x * cos.reshape(bshape).astype(x.dtype) + xr * sin.reshape(bshape).astype(x.dtype)


def map_query_blocks(fn, q):
    B, S = q.shape[:2]
    nb = S // Q_BLOCK
    qb = jnp.moveaxis(q.reshape((B, nb, Q_BLOCK) + q.shape[2:]), 1, 0)
    out = jnp.moveaxis(lax.map(fn, qb), 0, 1)
    return out.reshape((B, S) + out.shape[3:])


def diff_attend(q, k, v, lam):
    scale = DIFF_HEAD_DIM ** -0.5

    def one(qb):
        s = jnp.einsum('bqhjd,bkhjd->bhjqk', qb, k, preferred_element_type=jnp.float32) * scale
        p = jax.nn.softmax(s, axis=-1)
        w = p[:, :, 0] - lam * p[:, :, 1]
        return jnp.einsum('bhqk,bkhe->bqhe', w.astype(v.dtype), v)

    return map_query_blocks(one, q)


def softmax_attend(q, k, v, scale):
    def one(qb):
        s = jnp.einsum('bqhd,bkhd->bhqk', qb, k, preferred_element_type=jnp.float32) * scale
        p = jax.nn.softmax(s, axis=-1)
        return jnp.einsum('bhqk,bkhe->bqhe', p.astype(v.dtype), v)

    return map_query_blocks(one, q)


def gmlp_chunk_mixer(h, w_in, b_in, ln_g, w_s, b_s, w_out):
    B, T, _ = h.shape
    z = jax.nn.gelu(h @ w_in + b_in)
    u, v = jnp.split(z, 2, axis=-1)
    v = layer_norm(v, ln_g)
    v = v.reshape(B, T // CHUNK, CHUNK, GMLP_GROUPS, GMLP_GROUP_DIM)
    sv = jnp.einsum('gpq,bnqgc->bnpgc', w_s, v) + b_s.T[:, :, None]
    return (u * sv.reshape(B, T, GMLP_WIDTH)) @ w_out


def diff_attention_mixer(h_lat, h_ctx, cos, sin, w_qkv, lam_q1, lam_k1, lam_q2, lam_k2,
                         subln_g, w_o, lambda_init, need_ctx_out):
    H, d = DIFF_HEADS, DIFF_HEAD_DIM

    def proj(h):
        B, T = h.shape[:2]
        q, k, v = jnp.split(h @ w_qkv, [2 * H * d, 4 * H * d], axis=-1)
        return q.reshape(B, T, H, 2, d), k.reshape(B, T, H, 2, d), v.reshape(B, T, H, DIFF_V_DIM)

    def post(o):
        B, T = o.shape[:2]
        o = rms_norm(o, subln_g) * (1.0 - lambda_init)
        return o.reshape(B, T, H * DIFF_V_DIM) @ w_o

    f32 = jnp.float32
    lam = (jnp.exp(jnp.sum(lam_q1.astype(f32) * lam_k1.astype(f32)))
           - jnp.exp(jnp.sum(lam_q2.astype(f32) * lam_k2.astype(f32))) + lambda_init)
    q_l, k_l, v_l = proj(h_lat)
    q_l = apply_axial_rope(q_l, cos, sin)
    k_l = apply_axial_rope(k_l, cos, sin)
    q_c, k_c, v_c = proj(h_ctx)
    k_all = jnp.concatenate([k_l, k_c], axis=1)
    v_all = jnp.concatenate([v_l, v_c], axis=1)
    y_lat = post(diff_attend(q_l, k_all, v_all, lam))
    y_ctx = post(diff_attend(q_c, k_c, v_c, lam)) if need_ctx_out else None
    return y_lat, y_ctx


def mla_mixer(h_lat, h_ctx, cos, sin, w_dqkv, q_norm_g, w_uq, kv_norm_g, w_ukv, w_o, need_ctx_out):
    H = MLA_HEADS

    def proj(h, rotary):
        B, T = h.shape[:2]
        c_q, c_kv, k_r = jnp.split(h @ w_dqkv, [MLA_Q_RANK, MLA_Q_RANK + MLA_KV_RANK], axis=-1)
        q = (rms_norm(c_q, q_norm_g) @ w_uq).reshape(B, T, H, MLA_QK)
        kv = (rms_norm(c_kv, kv_norm_g) @ w_ukv).reshape(B, T, H, MLA_NOPE + MLA_V)
        q_n, q_r = jnp.split(q, [MLA_NOPE], axis=-1)
        k_n, v = jnp.split(kv, [MLA_NOPE], axis=-1)
        k_r = k_r[:, :, None, :]
        if rotary:
            q_r = apply_axial_rope(q_r, cos, sin)
            k_r = apply_axial_rope(k_r, cos, sin)
        q = jnp.concatenate([q_n, q_r], axis=-1)
        k = jnp.concatenate([k_n, jnp.broadcast_to(k_r, (B, T, H, MLA_ROPE))], axis=-1)
        return q, k, v

    def post(o):
        B, T = o.shape[:2]
        return o.reshape(B, T, H * MLA_V) @ w_o

    scale = MLA_QK ** -0.5
    q_l, k_l, v_l = proj(h_lat, True)
    q_c, k_c, v_c = proj(h_ctx, False)
    k_all = jnp.concatenate([k_l, k_c], axis=1)
    v_all = jnp.concatenate([v_l, v_c], axis=1)
    y_lat = post(softmax_attend(q_l, k_all, v_all, scale))
    y_ctx = post(softmax_attend(q_c, k_c, v_c, scale)) if need_ctx_out else None
    return y_lat, y_ctx


def sq_relu_mlp(h, w1, w2):
    return jnp.square(jax.nn.relu(h @ w1)) @ w2


def setup_inputs(seed: int = 0) -> dict:
    key = jax.random.key(seed)
    ks = iter(jax.random.split(key, 32))

    def nrm(shape, scale):
        return jax.random.normal(next(ks), shape, jnp.float32) * scale

    def gain(shape):
        return 1.0 + nrm(shape, 0.02)

    D = D_MODEL
    return {
        "x": nrm((BATCH, SEQ, D), 1.0),
        "c": nrm((BATCH, D), 1.0),
        "ctx": nrm((BATCH, CTX_LEN, D), 1.0),
        "c_ctx": nrm((D,), 1.0),
        "w_mod": nrm((DEPTH, D, N_MOD * D), 0.5 * D ** -0.5),
        "b_mod": nrm((DEPTH, N_MOD * D), 0.02),
        "norm1_g": gain((DEPTH, D)),
        "norm2_g": gain((DEPTH, D)),
        "w_ff1": nrm((DEPTH, D, D_FF), D ** -0.5),
        "w_ff2": nrm((DEPTH, D_FF, D), D_FF ** -0.5),
        "a_w_in": nrm((N_A, D, 2 * GMLP_WIDTH), D ** -0.5),
        "a_b_in": nrm((N_A, 2 * GMLP_WIDTH), 0.02),
        "a_ln_g": gain((N_A, GMLP_WIDTH)),
        "a_w_s": nrm((N_A, GMLP_GROUPS, CHUNK, CHUNK), CHUNK ** -0.5),
        "a_b_s": gain((N_A, GMLP_GROUPS, CHUNK)),
        "a_w_out": nrm((N_A, GMLP_WIDTH, D), GMLP_WIDTH ** -0.5),
        "b_w_qkv": nrm((N_B, D, 4 * DIFF_HEADS * DIFF_HEAD_DIM + DIFF_HEADS * DIFF_V_DIM), D ** -0.5),
        "b_lam_q1": nrm((N_B, DIFF_HEAD_DIM), 0.1),
        "b_lam_k1": nrm((N_B, DIFF_HEAD_DIM), 0.1),
        "b_lam_q2": nrm((N_B, DIFF_HEAD_DIM), 0.1),
        "b_lam_k2": nrm((N_B, DIFF_HEAD_DIM), 0.1),
        "b_subln_g": gain((N_B, DIFF_V_DIM)),
        "b_w_o": nrm((N_B, DIFF_HEADS * DIFF_V_DIM, D), (DIFF_HEADS * DIFF_V_DIM) ** -0.5),
        "c_w_dqkv": nrm((N_C, D, MLA_Q_RANK + MLA_KV_RANK + MLA_ROPE), D ** -0.5),
        "c_q_norm_g": gain((N_C, MLA_Q_RANK)),
        "c_w_uq": nrm((N_C, MLA_Q_RANK, MLA_HEADS * MLA_QK), MLA_Q_RANK ** -0.5),
        "c_kv_norm_g": gain((N_C, MLA_KV_RANK)),
        "c_w_ukv": nrm((N_C, MLA_KV_RANK, MLA_HEADS * (MLA_NOPE + MLA_V)), MLA_KV_RANK ** -0.5),
        "c_w_o": nrm((N_C, MLA_HEADS * MLA_V, D), (MLA_HEADS * MLA_V) ** -0.5),
        "final_g": gain((D,)),
    }


def reference(x, c, ctx, c_ctx, w_mod, b_mod, norm1_g, norm2_g, w_ff1, w_ff2,
              a_w_in, a_b_in, a_ln_g, a_w_s, a_b_s, a_w_out,
              b_w_qkv, b_lam_q1, b_lam_k1, b_lam_q2, b_lam_k2, b_subln_g, b_w_o,
              c_w_dqkv, c_q_norm_g, c_w_uq, c_kv_norm_g, c_w_ukv, c_w_o, final_g):
    B, S, D = x.shape
    ROWS = S // GRID_W
    rows = jnp.repeat(jnp.arange(ROWS, dtype=jnp.int32), GRID_W)
    cols = jnp.tile(jnp.arange(GRID_W, dtype=jnp.int32), ROWS)
    cos_b, sin_b = axial_rope_tables(rows, cols, DIFF_HEAD_DIM)
    cos_c, sin_c = axial_rope_tables(rows, cols, MLA_ROPE)
    s_c = jax.nn.silu(c)
    s_cc = jax.nn.silu(c_ctx)
    xc = ctx

    for l in range(DEPTH):
        kind = l % N_MIXERS
        idx = l // N_MIXERS
        ctx_later = any(j % N_MIXERS != 0 for j in range(l + 1, DEPTH))
        ctx_here = ctx_later or kind != 0

        sh1, sc1, g1, sh2, sc2, g2 = jnp.split((s_c @ w_mod[l] + b_mod[l])[:, None, :], N_MOD, axis=-1)
        hx = modulate(rms_norm(x, norm1_g[l]), sh1, sc1)
        hc = None
        if ctx_here:
            csh1, csc1, cg1, csh2, csc2, cg2 = jnp.split(s_cc @ w_mod[l] + b_mod[l], N_MOD, axis=-1)
            hc = modulate(rms_norm(xc, norm1_g[l]), csh1, csc1)

        if kind == 0:
            p = (a_w_in[idx], a_b_in[idx], a_ln_g[idx], a_w_s[idx], a_b_s[idx], a_w_out[idx])
            y_x = gmlp_chunk_mixer(hx, *p)
            y_c = gmlp_chunk_mixer(hc, *p) if ctx_later else None
        elif kind == 1:
            lambda_init = 0.8 - 0.6 * math.exp(-0.3 * l)
            y_x, y_c = diff_attention_mixer(hx, hc, cos_b, sin_b, b_w_qkv[idx], b_lam_q1[idx], b_lam_k1[idx],
                                            b_lam_q2[idx], b_lam_k2[idx], b_subln_g[idx], b_w_o[idx],
                                            lambda_init, ctx_later)
        else:
            y_x, y_c = mla_mixer(hx, hc, cos_c, sin_c, c_w_dqkv[idx], c_q_norm_g[idx], c_w_uq[idx],
                                 c_kv_norm_g[idx], c_w_ukv[idx], c_w_o[idx], ctx_later)

        x = x + g1 * y_x
        x = x + g2 * sq_relu_mlp(modulate(rms_norm(x, norm2_g[l]), sh2, sc2), w_ff1[l], w_ff2[l])
        if ctx_later:
            xc = xc + cg1 * y_c
            xc = xc + cg2 * sq_relu_mlp(modulate(rms_norm(xc, norm2_g[l]), csh2, csc2), w_ff1[l], w_ff2[l])

    return rms_norm(x, final_g)
```

```python
import functools
import math

import jax
import jax.numpy as jnp
from jax import lax
from jax.experimental import pallas as pl
from jax.experimental.pallas import tpu as pltpu

D = 2048
BATCH = 2
SEQ = 8192
DEPTH = 4
GRID_W = 64
CTX = 256
N_MOD = 6
D_FF = 4 * D
EPS = 1e-6
ROPE_BASE = 10000.0

CHUNK = 128
GROUPS = 16
GDIM = 128

DIFF_H = 8
DIFF_D = 128
DIFF_V = 256

MLA_H = 16
MLA_QR = 448
MLA_QR_PAD = 512
MLA_KVR = 512
MLA_NOPE = 128
MLA_ROPE = 64
MLA_V = 128
MLA_QK = MLA_NOPE + MLA_ROPE
MLA_QK_PAD = 256
MLA_C_PAD = MLA_QR_PAD + MLA_KVR + 128

T_LAT = BATCH * SEQ
T_CTX = BATCH * CTX
T_ALL = T_LAT + T_CTX

TM = 512
TILES_PER_BATCH = SEQ // TM
N_LAT_TILES = T_LAT // TM
N_ALL_TILES = T_ALL // TM
ROPE_ID_TILE = SEQ // TM

VMEM_LIMIT = 56 * 1024 * 1024

BF16 = jnp.bfloat16
F32 = jnp.float32


def _cparams(sem):
    return pltpu.CompilerParams(dimension_semantics=sem, vmem_limit_bytes=VMEM_LIMIT)


def _mod_row(i):
    return jnp.minimum(i // TILES_PER_BATCH, 2)


def _mod_spec(layer, k):
    return pl.BlockSpec((None, None, 1, D), lambda i, *_: (layer, _mod_row(i), 0, k))


def _rope_tile(i):
    return jnp.where(i < N_LAT_TILES, i % TILES_PER_BATCH, ROPE_ID_TILE)


def _norm_mod(x, g, sh, sc):
    y = x * lax.rsqrt(jnp.mean(x * x, axis=-1, keepdims=True) + EPS)
    y = y * g
    return y * (1.0 + sc) + sh


def _rope(x, cos, sin_a, sin_b, ch):
    return (x * cos + pltpu.roll(x, 128 - ch, 1) * sin_a + pltpu.roll(x, ch, 1) * sin_b)


def _mod_kernel(c_ref, w_ref, b_ref, o_ref):
    c = c_ref[...]
    s = (c * jax.nn.sigmoid(c)).astype(BF16)
    o_ref[...] = jnp.dot(s, w_ref[...].astype(BF16), preferred_element_type=F32) + b_ref[...]


def _modulation(cc, w_mod, b_mod):
    tn = 1024
    n = N_MOD * D
    return pl.pallas_call(
        _mod_kernel,
        grid=(DEPTH, n // tn),
        in_specs=[
            pl.BlockSpec((8, D), lambda l, j: (0, 0)),
            pl.BlockSpec((None, D, tn), lambda l, j: (l, 0, j)),
            pl.BlockSpec((None, 1, tn), lambda l, j: (l, 0, j)),
        ],
        out_specs=pl.BlockSpec((None, 8, tn), lambda l, j: (l, 0, j)),
        out_shape=jax.ShapeDtypeStruct((DEPTH, 8, n), F32),
        compiler_params=_cparams(("arbitrary", "arbitrary")),
        name="modulation",
    )(cc, w_mod, b_mod.reshape(DEPTH, 1, n))


def _gelu_tanh(x):
    return 0.5 * x * (1.0 + jnp.tanh(math.sqrt(2.0 / math.pi) * (x + 0.044715 * (x * x * x))))


def _gmlp_kernel(x_ref, g_ref, sh_ref, sc_ref, win_ref, bin_ref, lng_ref, ws_ref, bs_ref, o_ref,
                 vn_ref):
    tm = x_ref.shape[0]
    nchunk = tm // CHUNK
    h = _norm_mod(x_ref[...], g_ref[...], sh_ref[...], sc_ref[...]).astype(BF16)
    zv = _gelu_tanh(jnp.dot(h, win_ref[:, D:], preferred_element_type=F32) + bin_ref[:, D:])
    mu = jnp.mean(zv, axis=-1, keepdims=True)
    zc = zv - mu
    vn = zc * lax.rsqrt(jnp.mean(zc * zc, axis=-1, keepdims=True) + EPS) * lng_ref[...]
    vn_ref[...] = vn.astype(BF16)
    ublk = 512
    for ub in range(D // ublk):
        zu = _gelu_tanh(jnp.dot(h, win_ref[:, ub * ublk:(ub + 1) * ublk], preferred_element_type=F32)
                        + bin_ref[:, ub * ublk:(ub + 1) * ublk])
        for gg in range(ublk // GDIM):
            grp = ub * (ublk // GDIM) + gg
            cols = slice(grp * GDIM, (grp + 1) * GDIM)
            rhs = jnp.concatenate([vn_ref[c * CHUNK:(c + 1) * CHUNK, cols] for c in range(nchunk)], axis=1)
            sv = jnp.dot(ws_ref[grp], rhs, preferred_element_type=F32)
            for c in range(nchunk):
                svc = sv[:, c * CHUNK:(c + 1) * CHUNK] + bs_ref[grp]
                u = zu[c * CHUNK:(c + 1) * CHUNK, gg * GDIM:(gg + 1) * GDIM]
                o_ref[c * CHUNK:(c + 1) * CHUNK, cols] = (u * svc).astype(BF16)


def _gmlp_mix(xs, mod, layer, n_tiles, norm_g, w_in, b_in, ln_g, w_s, b_s):
    tm = 256
    scale = TM // tm
    const2 = lambda i: (0, 0)
    mod_spec = lambda k: pl.BlockSpec((None, None, 1, D), lambda i: (layer, _mod_row(i // scale), 0, k))
    return pl.pallas_call(
        _gmlp_kernel,
        grid=(n_tiles * scale,),
        in_specs=[
            pl.BlockSpec((tm, D), lambda i: (i, 0)),
            pl.BlockSpec((1, D), const2),
            mod_spec(0), mod_spec(1),
            pl.BlockSpec((D, 2 * D), const2),
            pl.BlockSpec((1, 2 * D), const2),
            pl.BlockSpec((1, D), const2),
            pl.BlockSpec((GROUPS, CHUNK, CHUNK), lambda i: (0, 0, 0)),
            pl.BlockSpec((GROUPS, CHUNK, CHUNK), lambda i: (0, 0, 0)),
        ],
        out_specs=pl.BlockSpec((tm, D), lambda i: (i, 0)),
        out_shape=jax.ShapeDtypeStruct((T_ALL, D), BF16),
        scratch_shapes=[pltpu.VMEM((tm, D), BF16)],
        compiler_params=_cparams(("arbitrary",)),
        name=f"gmlp_mix_l{layer}",
    )(xs, norm_g.reshape(1, D), mod, mod, w_in, b_in.reshape(1, 2 * D), ln_g.reshape(1, D), w_s, b_s)


def _out_proj_kernel(x_ref, a_ref, w_ref, gate_ref, o_ref):
    y = jnp.dot(a_ref[...], w_ref[...], preferred_element_type=F32)
    o_ref[...] = x_ref[...] + gate_ref[...] * y


def _out_proj(xs, a, w, mod, layer, n_tiles):
    ka = a.shape[1]
    return pl.pallas_call(
        _out_proj_kernel,
        grid=(n_tiles,),
        in_specs=[
            pl.BlockSpec((TM, D), lambda i: (i, 0)),
            pl.BlockSpec((TM, ka), lambda i: (i, 0)),
            pl.BlockSpec((ka, D), lambda i: (0, 0)),
            _mod_spec(layer, 2),
        ],
        out_specs=pl.BlockSpec((TM, D), lambda i: (i, 0)),
        out_shape=jax.ShapeDtypeStruct((T_ALL, D), F32),
        input_output_aliases={0: 0},
        compiler_params=_cparams(("arbitrary",)),
        name=f"out_proj_l{layer}",
    )(xs, a, w, mod)


def _ffn_kernel(x_ref, g_ref, sh_ref, sc_ref, gate_ref, w1_ref, w2_ref, fg_ref, o_ref, h_ref, *,
                n_f, final_norm):
    f = pl.program_id(1)

    @pl.when(f == 0)
    def _():
        h_ref[...] = _norm_mod(x_ref[...], g_ref[...], sh_ref[...], sc_ref[...]).astype(BF16)
        o_ref[...] = jnp.zeros_like(o_ref)

    a = jnp.dot(h_ref[...], w1_ref[...], preferred_element_type=F32)
    a = jnp.square(jnp.maximum(a, 0.0)).astype(BF16)
    o_ref[...] += jnp.dot(a, w2_ref[...], preferred_element_type=F32)

    @pl.when(f == n_f - 1)
    def _():
        y = x_ref[...] + gate_ref[...] * o_ref[...]
        if final_norm:
            y = y * lax.rsqrt(jnp.mean(y * y, axis=-1, keepdims=True) + EPS) * fg_ref[...]
        o_ref[...] = y


def _ffn(xs, mod, layer, n_tiles, norm_g, w1, w2, final_g, final_norm):
    tf = 512
    n_f = D_FF // tf
    rows = T_LAT if final_norm else T_ALL
    const2 = lambda i, f: (0, 0)
    return pl.pallas_call(
        functools.partial(_ffn_kernel, n_f=n_f, final_norm=final_norm),
        grid=(n_tiles, n_f),
        in_specs=[
            pl.BlockSpec((TM, D), lambda i, f: (i, 0)),
            pl.BlockSpec((1, D), const2),
            _mod_spec(layer, 3), _mod_spec(layer, 4), _mod_spec(layer, 5),
            pl.BlockSpec((D, tf), lambda i, f: (0, f)),
            pl.BlockSpec((tf, D), lambda i, f: (f, 0)),
            pl.BlockSpec((1, D), const2),
        ],
        out_specs=pl.BlockSpec((TM, D), lambda i, f: (i, 0)),
        out_shape=jax.ShapeDtypeStruct((rows, D), F32),
        input_output_aliases={} if final_norm else {0: 0},
        scratch_shapes=[pltpu.VMEM((TM, D), BF16)],
        compiler_params=_cparams(("arbitrary", "arbitrary")),
        name=f"ffn_l{layer}",
    )(xs, norm_g.reshape(1, D), mod, mod, mod, w1, w2, final_g.reshape(1, D))


DIFF_QK_COLS = 4 * DIFF_H * DIFF_D
DIFF_N = DIFF_QK_COLS + DIFF_H * DIFF_V


def _diff_proj_kernel(x_ref, g_ref, sh_ref, sc_ref, w_ref, cos_ref, sa_ref, sb_ref, o_ref, h_ref, *,
                      tn):
    n = pl.program_id(1)

    @pl.when(n == 0)
    def _():
        h_ref[...] = _norm_mod(x_ref[...], g_ref[...], sh_ref[...], sc_ref[...]).astype(BF16)

    y = jnp.dot(h_ref[...], w_ref[...], preferred_element_type=F32)

    @pl.when(n < DIFF_QK_COLS // tn)
    def _():
        cos, sa, sb = cos_ref[...], sa_ref[...], sb_ref[...]
        for s in range(tn // 128):
            o_ref[:, s * 128:(s + 1) * 128] = _rope(y[:, s * 128:(s + 1) * 128], cos, sa, sb, 32).astype(BF16)

    @pl.when(n >= DIFF_QK_COLS // tn)
    def _():
        o_ref[...] = y.astype(BF16)


def _diff_proj(xs, mod, layer, norm_g, w_qkv, tabs):
    tn = 512
    const2 = lambda i, n: (0, 0)
    tab_spec = pl.BlockSpec((TM, 128), lambda i, n: (_rope_tile(i), 0))
    return pl.pallas_call(
        functools.partial(_diff_proj_kernel, tn=tn),
        grid=(N_ALL_TILES, DIFF_N // tn),
        in_specs=[
            pl.BlockSpec((TM, D), lambda i, n: (i, 0)),
            pl.BlockSpec((1, D), const2),
            _mod_spec(layer, 0), _mod_spec(layer, 1),
            pl.BlockSpec((D, tn), lambda i, n: (0, n)),
            tab_spec, tab_spec, tab_spec,
        ],
        out_specs=pl.BlockSpec((TM, tn), lambda i, n: (i, n)),
        out_shape=jax.ShapeDtypeStruct((T_ALL, DIFF_N), BF16),
        scratch_shapes=[pltpu.VMEM((TM, D), BF16)],
        compiler_params=_cparams(("arbitrary", "arbitrary")),
        name=f"diff_proj_l{layer}",
    )(xs, norm_g.reshape(1, D), mod, mod, w_qkv, *tabs)


def _softmax_step(s, v, m_ref, l_ref, acc_ref):
    m_prev = m_ref[...]
    m_new = jnp.maximum(m_prev, jnp.max(s, axis=-1, keepdims=True))
    alpha = jnp.exp(m_prev - m_new)
    p = jnp.exp(s - m_new)
    l_ref[...] = alpha * l_ref[...] + jnp.sum(p, axis=-1, keepdims=True)
    acc_ref[...] = alpha * acc_ref[...] + jnp.dot(p.astype(BF16), v, preferred_element_type=F32)
    m_ref[...] = m_new


_NT = (((1,), (1,)), ((), ()))


def _diff_attn_kernel(*refs, has_lat, n_kv, scale, lambda_init):
    if has_lat:
        (q_ref, kc_ref, vc_ref, kl_ref, vl_ref, lam_ref, sg_ref, o_ref,
         m1, l1, a1, m2, l2, a2) = refs
    else:
        (q_ref, kc_ref, vc_ref, lam_ref, sg_ref, o_ref, m1, l1, a1, m2, l2, a2) = refs
        kl_ref = vl_ref = None
    j = pl.program_id(3)

    def step(k_ref, v_ref):
        q = q_ref[...]
        k = k_ref[...]
        v = v_ref[...]
        s1 = lax.dot_general(q[:, :DIFF_D], k[:, :DIFF_D], _NT, preferred_element_type=F32) * scale
        _softmax_step(s1, v, m1, l1, a1)
        s2 = lax.dot_general(q[:, DIFF_D:], k[:, DIFF_D:], _NT, preferred_element_type=F32) * scale
        _softmax_step(s2, v, m2, l2, a2)

    @pl.when(j == 0)
    def _():
        for m, l, a in ((m1, l1, a1), (m2, l2, a2)):
            m[...] = jnp.full_like(m, -jnp.inf)
            l[...] = jnp.zeros_like(l)
            a[...] = jnp.zeros_like(a)
        step(kc_ref, vc_ref)

    if has_lat:
        step(kl_ref, vl_ref)

    @pl.when(j == n_kv - 1)
    def _():
        lam_v = lam_ref[...]
        lam = (jnp.exp(jnp.sum(lam_v[0:1] * lam_v[1:2], axis=-1, keepdims=True))
               - jnp.exp(jnp.sum(lam_v[2:3] * lam_v[3:4], axis=-1, keepdims=True)) + lambda_init)
        o = a1[...] / l1[...] - lam * (a2[...] / l2[...])
        o = o * lax.rsqrt(jnp.mean(o * o, axis=-1, keepdims=True) + EPS) * sg_ref[...]
        o_ref[...] = (o * (1.0 - lambda_init)).astype(BF16)


def _diff_attn(qkv, lam_tab, subln_g, lambda_init, o_prev):
    has_lat = o_prev is None
    scale = DIFF_D ** -0.5
    kcol, vcol = 2 * DIFF_H * DIFF_D // DIFF_V, 4 * DIFF_H * DIFF_D // DIFF_V
    ctx_blk0 = T_LAT // CTX
    if has_lat:
        tq, tk = 512, 512
        n_kv = SEQ // tk
        grid = (BATCH, DIFF_H, SEQ // tq, n_kv)
        q_row = lambda b, h, i, j: b * (SEQ // tq) + i
    else:
        tq, tk = CTX, CTX
        n_kv = 1
        grid = (BATCH, DIFF_H, 1, 1)
        q_row = lambda b, h, i, j: ctx_blk0 + b
    in_specs = [
        pl.BlockSpec((tq, DIFF_V), lambda b, h, i, j: (q_row(b, h, i, j), h)),
        pl.BlockSpec((CTX, DIFF_V), lambda b, h, i, j: (ctx_blk0 + b, kcol + h)),
        pl.BlockSpec((CTX, DIFF_V), lambda b, h, i, j: (ctx_blk0 + b, vcol + h)),
    ]
    args = [qkv, qkv, qkv]
    if has_lat:
        in_specs += [
            pl.BlockSpec((tk, DIFF_V), lambda b, h, i, j: (b * (SEQ // tk) + j, kcol + h)),
            pl.BlockSpec((tk, DIFF_V), lambda b, h, i, j: (b * (SEQ // tk) + j, vcol + h)),
        ]
        args += [qkv, qkv]
    in_specs += [
        pl.BlockSpec((8, DIFF_D), lambda b, h, i, j: (0, 0)),
        pl.BlockSpec((1, DIFF_V), lambda b, h, i, j: (0, 0)),
    ]
    args += [lam_tab, subln_g.reshape(1, DIFF_V)]
    aliases = {}
    if not has_lat:
        in_specs.append(pl.BlockSpec(memory_space=pl.ANY))
        args.append(o_prev)
        aliases = {len(args) - 1: 0}
    kern = functools.partial(_diff_attn_kernel, has_lat=has_lat, n_kv=n_kv, scale=scale,
                             lambda_init=lambda_init)
    if not has_lat:
        base = kern
        kern = lambda *refs: base(*refs[:5], *refs[6:])
    return pl.pallas_call(
        kern,
        grid=grid,
        in_specs=in_specs,
        out_specs=pl.BlockSpec((tq, DIFF_V), lambda b, h, i, j: (q_row(b, h, i, j), h)),
        out_shape=jax.ShapeDtypeStruct((T_ALL, DIFF_H * DIFF_V), BF16),
        input_output_aliases=aliases,
        scratch_shapes=[pltpu.VMEM((tq, 1), F32), pltpu.VMEM((tq, 1), F32), pltpu.VMEM((tq, DIFF_V), F32),
                        pltpu.VMEM((tq, 1), F32), pltpu.VMEM((tq, 1), F32), pltpu.VMEM((tq, DIFF_V), F32)],
        compiler_params=_cparams(("arbitrary",) * 4),
        name="diff_attn_lat" if has_lat else "diff_attn_ctx",
    )(*args)


def _mla_attn_kernel(q_ref, kc_ref, vc_ref, kl_ref, vl_ref, o_ref, m, l, a, *, n_kv, scale):
    j = pl.program_id(3)

    def step(k_ref, v_ref):
        s = lax.dot_general(q_ref[...], k_ref[...], _NT, preferred_element_type=F32) * scale
        _softmax_step(s, v_ref[...], m, l, a)

    @pl.when(j == 0)
    def _():
        m[...] = jnp.full_like(m, -jnp.inf)
        l[...] = jnp.zeros_like(l)
        a[...] = jnp.zeros_like(a)
        step(kc_ref, vc_ref)

    step(kl_ref, vl_ref)

    @pl.when(j == n_kv - 1)
    def _():
        o_ref[...] = (a[...] / l[...]).astype(BF16)


def _mla_attn(q, k, v):
    tq, tk = 512, 512
    n_kv = SEQ // tk
    scale = MLA_QK ** -0.5
    ctx_blk0 = T_LAT // CTX
    return pl.pallas_call(
        functools.partial(_mla_attn_kernel, n_kv=n_kv, scale=scale),
        grid=(BATCH, MLA_H, SEQ // tq, n_kv),
        in_specs=[
            pl.BlockSpec((tq, MLA_QK_PAD), lambda b, h, i, j: (b * (SEQ // tq) + i, h)),
            pl.BlockSpec((CTX, MLA_QK_PAD), lambda b, h, i, j: (ctx_blk0 + b, h)),
            pl.BlockSpec((CTX, MLA_V), lambda b, h, i, j: (ctx_blk0 + b, h)),
            pl.BlockSpec((tk, MLA_QK_PAD), lambda b, h, i, j: (b * (SEQ // tk) + j, h)),
            pl.BlockSpec((tk, MLA_V), lambda b, h, i, j: (b * (SEQ // tk) + j, h)),
        ],
        out_specs=pl.BlockSpec((tq, MLA_V), lambda b, h, i, j: (b * (SEQ // tq) + i, h)),
        out_shape=jax.ShapeDtypeStruct((T_LAT, MLA_H * MLA_V), BF16),
        scratch_shapes=[pltpu.VMEM((tq, 1), F32), pltpu.VMEM((tq, 1), F32), pltpu.VMEM((tq, MLA_V), F32)],
        compiler_params=_cparams(("arbitrary",) * 4),
        name="mla_attn",
    )(q, k, v, k, v)


def _mla_proj_kernel(x_ref, g_ref, sh_ref, sc_ref, wd_ref, qg_ref, kvg_ref, wuq_ref, wukv_ref,
                     cos_ref, sa_ref, sb_ref, q_ref, k_ref, v_ref):
    h = _norm_mod(x_ref[...], g_ref[...], sh_ref[...], sc_ref[...]).astype(BF16)
    c = jnp.dot(h, wd_ref[...], preferred_element_type=F32)
    cq = c[:, :MLA_QR_PAD]
    ckv = c[:, MLA_QR_PAD:MLA_QR_PAD + MLA_KVR]
    kr = c[:, MLA_QR_PAD + MLA_KVR:]
    cqn = cq * lax.rsqrt(jnp.sum(cq * cq, axis=-1, keepdims=True) * (1.0 / MLA_QR) + EPS) * qg_ref[...]
    ckvn = ckv * lax.rsqrt(jnp.mean(ckv * ckv, axis=-1, keepdims=True) + EPS) * kvg_ref[...]
    cos, sa, sb = cos_ref[...], sa_ref[...], sb_ref[...]
    kr_rot = _rope(kr, cos, sa, sb, 16).astype(BF16)
    q = jnp.dot(cqn.astype(BF16), wuq_ref[...], preferred_element_type=F32)
    kv = jnp.dot(ckvn.astype(BF16), wukv_ref[...], preferred_element_type=F32)
    for hd in range(MLA_H):
        base = hd * MLA_QK_PAD
        q_ref[:, base:base + 128] = q[:, base:base + 128].astype(BF16)
        q_ref[:, base + 128:base + 256] = _rope(q[:, base + 128:base + 256], cos, sa, sb, 16).astype(BF16)
        k_ref[:, base:base + 128] = kv[:, hd * MLA_NOPE:(hd + 1) * MLA_NOPE].astype(BF16)
        k_ref[:, base + 128:base + 256] = kr_rot
    v_ref[...] = kv[:, MLA_H * MLA_NOPE:].astype(BF16)


def _mla_proj(xs, mod, layer, norm_g, w_d, q_g, kv_g, w_uq, w_ukv, tabs):
    tm = 256
    scale = TM // tm
    const2 = lambda i: (0, 0)
    mod_spec = lambda k: pl.BlockSpec((None, None, 1, D), lambda i: (layer, _mod_row(i // scale), 0, k))
    tab_spec = pl.BlockSpec((tm, 128), lambda i: (
        jnp.where(i < N_LAT_TILES * scale, i % (TILES_PER_BATCH * scale), TILES_PER_BATCH * scale), 0))
    n_q = MLA_H * MLA_QK_PAD
    return pl.pallas_call(
        _mla_proj_kernel,
        grid=(N_ALL_TILES * scale,),
        in_specs=[
            pl.BlockSpec((tm, D), lambda i: (i, 0)),
            pl.BlockSpec((1, D), const2),
            mod_spec(0), mod_spec(1),
            pl.BlockSpec((D, MLA_C_PAD), const2),
            pl.BlockSpec((1, MLA_QR_PAD), const2),
            pl.BlockSpec((1, MLA_KVR), const2),
            pl.BlockSpec((MLA_QR_PAD, n_q), const2),
            pl.BlockSpec((MLA_KVR, MLA_H * (MLA_NOPE + MLA_V)), const2),
            tab_spec, tab_spec, tab_spec,
        ],
        out_specs=[
            pl.BlockSpec((tm, n_q), lambda i: (i, 0)),
            pl.BlockSpec((tm, n_q), lambda i: (i, 0)),
            pl.BlockSpec((tm, MLA_H * MLA_V), lambda i: (i, 0)),
        ],
        out_shape=[
            jax.ShapeDtypeStruct((T_ALL, n_q), BF16),
            jax.ShapeDtypeStruct((T_ALL, n_q), BF16),
            jax.ShapeDtypeStruct((T_ALL, MLA_H * MLA_V), BF16),
        ],
        compiler_params=_cparams(("arbitrary",)),
        name=f"mla_proj_l{layer}",
    )(xs, norm_g.reshape(1, D), mod, mod, w_d, q_g, kv_g, w_uq, w_ukv, *tabs)


def _rope_tables(rot_dim):
    a = rot_dim // 2
    ch = a // 2
    pos = jnp.arange(SEQ, dtype=jnp.int32)
    rows = (pos // GRID_W).astype(F32)
    cols = (pos % GRID_W).astype(F32)
    inv = ROPE_BASE ** (-jnp.arange(0, a, 2, dtype=F32) / a)

    def axis_angles(p):
        ang = p[:, None] * inv[None, :]
        return jnp.concatenate([ang, ang], axis=-1)

    ang = jnp.concatenate([axis_angles(rows), axis_angles(cols)], axis=-1)
    cos, sin = jnp.cos(ang), jnp.sin(ang)
    lane = jnp.arange(rot_dim)
    first = (lane % a) < ch
    sin_a = jnp.where(first[None, :], -sin, 0.0)
    sin_b = jnp.where(first[None, :], 0.0, sin)
    pad = 128 - rot_dim
    cos = jnp.pad(cos, ((0, TM), (0, pad)), constant_values=1.0)
    cos = cos.at[SEQ:, :].set(1.0)
    sin_a = jnp.pad(sin_a, ((0, TM), (0, pad)))
    sin_b = jnp.pad(sin_b, ((0, TM), (0, pad)))
    return cos, sin_a, sin_b


def kernel(x, c, ctx, c_ctx, w_mod, b_mod, norm1_g, norm2_g, w_ff1, w_ff2, a_w_in, a_b_in, a_ln_g, a_w_s, a_b_s, a_w_out, b_w_qkv, b_lam_q1, b_lam_k1, b_lam_q2, b_lam_k2, b_subln_g, b_w_o, c_w_dqkv, c_q_norm_g, c_w_uq, c_kv_norm_g, c_w_ukv, c_w_o, final_g):
    xs = jnp.concatenate([x.reshape(T_LAT, D), ctx.reshape(T_CTX, D)], axis=0)
    cc = jnp.concatenate([c, c_ctx[None, :], jnp.zeros((8 - BATCH - 1, D), F32)], axis=0)
    mod = _modulation(cc, w_mod, b_mod).reshape(DEPTH, 8, 1, N_MOD * D)

    tabs_b = _rope_tables(DIFF_D)
    tabs_c = _rope_tables(MLA_ROPE)

    out = None
    for l in range(DEPTH):
        kind, idx = l % 3, l // 3
        ctx_later = any(j % 3 != 0 for j in range(l + 1, DEPTH))
        last = l == DEPTH - 1
        n_tiles = N_ALL_TILES if ctx_later else N_LAT_TILES
        if kind == 0:
            b_s = jnp.broadcast_to(a_b_s[idx][:, :, None], (GROUPS, CHUNK, CHUNK))
            t = _gmlp_mix(xs, mod, l, n_tiles, norm1_g[l], a_w_in[idx].astype(BF16), a_b_in[idx],
                          a_ln_g[idx], a_w_s[idx].astype(BF16), b_s)
            xs = _out_proj(xs, t, a_w_out[idx].astype(BF16), mod, l, n_tiles)
        elif kind == 1:
            lambda_init = 0.8 - 0.6 * math.exp(-0.3 * l)
            qkv = _diff_proj(xs, mod, l, norm1_g[l], b_w_qkv[idx].astype(BF16), tabs_b)
            lam_tab = jnp.concatenate([b_lam_q1[idx][None], b_lam_k1[idx][None], b_lam_q2[idx][None],
                                       b_lam_k2[idx][None], jnp.zeros((4, DIFF_D), F32)], axis=0)
            o = _diff_attn(qkv, lam_tab, b_subln_g[idx], lambda_init, None)
            if ctx_later:
                o = _diff_attn(qkv, lam_tab, b_subln_g[idx], lambda_init, o)
            xs = _out_proj(xs, o, b_w_o[idx].astype(BF16), mod, l, n_tiles)
        else:
            w_d = c_w_dqkv[idx]
            w_d = jnp.concatenate([
                jnp.pad(w_d[:, :MLA_QR], ((0, 0), (0, MLA_QR_PAD - MLA_QR))),
                w_d[:, MLA_QR:MLA_QR + MLA_KVR],
                jnp.pad(w_d[:, MLA_QR + MLA_KVR:], ((0, 0), (0, 128 - MLA_ROPE)))], axis=1).astype(BF16)
            w_uq = jnp.pad(c_w_uq[idx].reshape(MLA_QR, MLA_H, MLA_QK),
                           ((0, MLA_QR_PAD - MLA_QR), (0, 0), (0, MLA_QK_PAD - MLA_QK)))
            w_uq = w_uq.reshape(MLA_QR_PAD, MLA_H * MLA_QK_PAD).astype(BF16)
            w_ukv = c_w_ukv[idx].reshape(MLA_KVR, MLA_H, MLA_NOPE + MLA_V)
            w_ukv = jnp.concatenate([w_ukv[:, :, :MLA_NOPE].reshape(MLA_KVR, MLA_H * MLA_NOPE),
                                     w_ukv[:, :, MLA_NOPE:].reshape(MLA_KVR, MLA_H * MLA_V)],
                                    axis=1).astype(BF16)
            q_g = jnp.pad(c_q_norm_g[idx], (0, MLA_QR_PAD - MLA_QR)).reshape(1, MLA_QR_PAD)
            q, k, v = _mla_proj(xs, mod, l, norm1_g[l], w_d, q_g, c_kv_norm_g[idx].reshape(1, MLA_KVR),
                                w_uq, w_ukv, tabs_c)
            o = _mla_attn(q, k, v)
            xs = _out_proj(xs, o, c_w_o[idx].astype(BF16), mod, l, N_LAT_TILES)
        if ctx_later and kind == 2:
            raise NotImplementedError("context update after an MLA layer is not needed at this depth")
        res = _ffn(xs, mod, l, n_tiles, norm2_g[l], w_ff1[l].astype(BF16), w_ff2[l].astype(BF16),
                   final_g, last)
        if last:
            out = res
        else:
            xs = res
    return out.reshape(BATCH, SEQ, D)
```

```python
import functools
import math

import jax
import jax.numpy as jnp
from jax import lax
from jax.experimental import pallas as pl
from jax.experimental.pallas import tpu as pltpu

D = 2048
BATCH = 2
SEQ = 8192
DEPTH = 4
GRID_W = 64
CTX = 256
N_MOD = 6
D_FF = 4 * D
EPS = 1e-6
ROPE_BASE = 10000.0

CHUNK = 128
GROUPS = 16
GDIM = 128

DIFF_H = 8
DIFF_D = 128
DIFF_V = 256

MLA_H = 16
MLA_QR = 448
MLA_QR_PAD = 512
MLA_KVR = 512
MLA_NOPE = 128
MLA_ROPE = 64
MLA_V = 128
MLA_QK = MLA_NOPE + MLA_ROPE
MLA_QK_PAD = 256
MLA_C_PAD = MLA_QR_PAD + MLA_KVR + 128

T_LAT = BATCH * SEQ
T_CTX = BATCH * CTX
T_ALL = T_LAT + T_CTX

TM = 512
TILES_PER_BATCH = SEQ // TM
N_LAT_TILES = T_LAT // TM
N_ALL_TILES = T_ALL // TM
ROPE_ID_TILE = SEQ // TM

VMEM_LIMIT = 56 * 1024 * 1024

BF16 = jnp.bfloat16
F32 = jnp.float32


def _cparams(sem, flags=None):
    return pltpu.CompilerParams(dimension_semantics=sem, vmem_limit_bytes=VMEM_LIMIT, flags=flags)


ATTN_FLAGS = None


def _mod_row(i):
    return jnp.minimum(i // TILES_PER_BATCH, 2)


def _mod_spec(layer, k):
    return pl.BlockSpec((None, None, 1, D), lambda i, *_: (layer, _mod_row(i), 0, k))


def _rope_tile(i):
    return jnp.where(i < N_LAT_TILES, i % TILES_PER_BATCH, ROPE_ID_TILE)


def _norm_mod(x, g, sh, sc):
    y = x * lax.rsqrt(jnp.mean(x * x, axis=-1, keepdims=True) + EPS)
    y = y * g
    return y * (1.0 + sc) + sh


def _rope(x, cos, sin_a, sin_b, ch):
    return (x * cos + pltpu.roll(x, 128 - ch, 1) * sin_a + pltpu.roll(x, ch, 1) * sin_b)


def _mod_kernel(c_ref, w_ref, b_ref, o_ref):
    c = c_ref[...]
    s = (c * jax.nn.sigmoid(c)).astype(BF16)
    o_ref[...] = jnp.dot(s, w_ref[...].astype(BF16), preferred_element_type=F32) + b_ref[...]


def _modulation(cc, w_mod, b_mod):
    tn = 1024
    n = N_MOD * D
    return pl.pallas_call(
        _mod_kernel,
        grid=(DEPTH, n // tn),
        in_specs=[
            pl.BlockSpec((8, D), lambda l, j: (0, 0)),
            pl.BlockSpec((None, D, tn), lambda l, j: (l, 0, j)),
            pl.BlockSpec((None, 1, tn), lambda l, j: (l, 0, j)),
        ],
        out_specs=pl.BlockSpec((None, 8, tn), lambda l, j: (l, 0, j)),
        out_shape=jax.ShapeDtypeStruct((DEPTH, 8, n), F32),
        compiler_params=_cparams(("arbitrary", "arbitrary")),
        name="modulation",
    )(cc, w_mod, b_mod.reshape(DEPTH, 1, n))


def _gelu_tanh(x):
    return 0.5 * x * (1.0 + jnp.tanh(math.sqrt(2.0 / math.pi) * (x + 0.044715 * (x * x * x))))


def _gmlp_kernel(x_ref, g_ref, sh_ref, sc_ref, win_ref, bin_ref, lng_ref, ws_ref, bs_ref, o_ref,
                 vn_ref):
    tm = x_ref.shape[0]
    nchunk = tm // CHUNK
    h = _norm_mod(x_ref[...], g_ref[...], sh_ref[...], sc_ref[...]).astype(BF16)
    zv = _gelu_tanh(jnp.dot(h, win_ref[:, D:], preferred_element_type=F32) + bin_ref[:, D:])
    mu = jnp.mean(zv, axis=-1, keepdims=True)
    zc = zv - mu
    vn = zc * lax.rsqrt(jnp.mean(zc * zc, axis=-1, keepdims=True) + EPS) * lng_ref[...]
    vn_ref[...] = vn.astype(BF16)
    ublk = 512
    for ub in range(D // ublk):
        zu = _gelu_tanh(jnp.dot(h, win_ref[:, ub * ublk:(ub + 1) * ublk], preferred_element_type=F32)
                        + bin_ref[:, ub * ublk:(ub + 1) * ublk])
        for gg in range(ublk // GDIM):
            grp = ub * (ublk // GDIM) + gg
            cols = slice(grp * GDIM, (grp + 1) * GDIM)
            rhs = jnp.concatenate([vn_ref[c * CHUNK:(c + 1) * CHUNK, cols] for c in range(nchunk)], axis=1)
            sv = jnp.dot(ws_ref[grp], rhs, preferred_element_type=F32)
            for c in range(nchunk):
                svc = sv[:, c * CHUNK:(c + 1) * CHUNK] + bs_ref[grp]
                u = zu[c * CHUNK:(c + 1) * CHUNK, gg * GDIM:(gg + 1) * GDIM]
                o_ref[c * CHUNK:(c + 1) * CHUNK, cols] = (u * svc).astype(BF16)


def _gmlp_mix(xs, mod, layer, n_tiles, norm_g, w_in, b_in, ln_g, w_s, b_s):
    tm = 256
    scale = TM // tm
    const2 = lambda i: (0, 0)
    mod_spec = lambda k: pl.BlockSpec((None, None, 1, D), lambda i: (layer, _mod_row(i // scale), 0, k))
    return pl.pallas_call(
        _gmlp_kernel,
        grid=(n_tiles * scale,),
        in_specs=[
            pl.BlockSpec((tm, D), lambda i: (i, 0)),
            pl.BlockSpec((1, D), const2),
            mod_spec(0), mod_spec(1),
            pl.BlockSpec((D, 2 * D), const2),
            pl.BlockSpec((1, 2 * D), const2),
            pl.BlockSpec((1, D), const2),
            pl.BlockSpec((GROUPS, CHUNK, CHUNK), lambda i: (0, 0, 0)),
            pl.BlockSpec((GROUPS, CHUNK, CHUNK), lambda i: (0, 0, 0)),
        ],
        out_specs=pl.BlockSpec((tm, D), lambda i: (i, 0)),
        out_shape=jax.ShapeDtypeStruct((T_ALL, D), BF16),
        scratch_shapes=[pltpu.VMEM((tm, D), BF16)],
        compiler_params=_cparams(("arbitrary",)),
        name=f"gmlp_mix_l{layer}",
    )(xs, norm_g.reshape(1, D), mod, mod, w_in, b_in.reshape(1, 2 * D), ln_g.reshape(1, D), w_s, b_s)


def _out_proj_kernel(x_ref, a_ref, w_ref, gate_ref, o_ref):
    y = jnp.dot(a_ref[...], w_ref[...], preferred_element_type=F32)
    o_ref[...] = x_ref[...] + gate_ref[...] * y


def _out_proj(xs, a, w, mod, layer, n_tiles):
    ka = a.shape[1]
    return pl.pallas_call(
        _out_proj_kernel,
        grid=(n_tiles,),
        in_specs=[
            pl.BlockSpec((TM, D), lambda i: (i, 0)),
            pl.BlockSpec((TM, ka), lambda i: (i, 0)),
            pl.BlockSpec((ka, D), lambda i: (0, 0)),
            _mod_spec(layer, 2),
        ],
        out_specs=pl.BlockSpec((TM, D), lambda i: (i, 0)),
        out_shape=jax.ShapeDtypeStruct((T_ALL, D), F32),
        input_output_aliases={0: 0},
        compiler_params=_cparams(("arbitrary",)),
        name=f"out_proj_l{layer}",
    )(xs, a, w, mod)


def _ffn_kernel(x_ref, g_ref, sh_ref, sc_ref, gate_ref, w1_ref, w2_ref, fg_ref, o_ref, h_ref, *,
                n_f, final_norm):
    f = pl.program_id(1)

    @pl.when(f == 0)
    def _():
        h_ref[...] = _norm_mod(x_ref[...], g_ref[...], sh_ref[...], sc_ref[...]).astype(BF16)
        o_ref[...] = jnp.zeros_like(o_ref)

    a = jnp.dot(h_ref[...], w1_ref[...], preferred_element_type=F32)
    a = jnp.square(jnp.maximum(a, 0.0)).astype(BF16)
    o_ref[...] += jnp.dot(a, w2_ref[...], preferred_element_type=F32)

    @pl.when(f == n_f - 1)
    def _():
        y = x_ref[...] + gate_ref[...] * o_ref[...]
        if final_norm:
            y = y * lax.rsqrt(jnp.mean(y * y, axis=-1, keepdims=True) + EPS) * fg_ref[...]
        o_ref[...] = y


def _ffn(xs, mod, layer, n_tiles, norm_g, w1, w2, final_g, final_norm):
    tf = 512
    n_f = D_FF // tf
    rows = T_LAT if final_norm else T_ALL
    const2 = lambda i, f: (0, 0)
    return pl.pallas_call(
        functools.partial(_ffn_kernel, n_f=n_f, final_norm=final_norm),
        grid=(n_tiles, n_f),
        in_specs=[
            pl.BlockSpec((TM, D), lambda i, f: (i, 0)),
            pl.BlockSpec((1, D), const2),
            _mod_spec(layer, 3), _mod_spec(layer, 4), _mod_spec(layer, 5),
            pl.BlockSpec((D, tf), lambda i, f: (0, f)),
            pl.BlockSpec((tf, D), lambda i, f: (f, 0)),
            pl.BlockSpec((1, D), const2),
        ],
        out_specs=pl.BlockSpec((TM, D), lambda i, f: (i, 0)),
        out_shape=jax.ShapeDtypeStruct((rows, D), F32),
        input_output_aliases={} if final_norm else {0: 0},
        scratch_shapes=[pltpu.VMEM((TM, D), BF16)],
        compiler_params=_cparams(("arbitrary", "arbitrary")),
        name=f"ffn_l{layer}",
    )(xs, norm_g.reshape(1, D), mod, mod, mod, w1, w2, final_g.reshape(1, D))


TB = 256


def _proj_kernel(x_ref, g_ref, sh_ref, sc_ref, w_ref, cos_ref, sa_ref, sb_ref, o_ref, h_ref, *,
                 rope, transpose):
    n = pl.program_id(1)

    @pl.when(n == 0)
    def _():
        h_ref[...] = _norm_mod(x_ref[...], g_ref[...], sh_ref[...], sc_ref[...]).astype(BF16)

    y = jnp.dot(h_ref[...], w_ref[...], preferred_element_type=F32)
    if rope:
        cos, sa, sb = cos_ref[...], sa_ref[...], sb_ref[...]
        y = jnp.concatenate([_rope(y[:, s * 128:(s + 1) * 128], cos, sa, sb, 32)
                             for s in range(y.shape[1] // 128)], axis=1)
    if transpose:
        for t in range(TM // TB):
            o_ref[t] = y[t * TB:(t + 1) * TB, :].T.astype(BF16)
    else:
        o_ref[...] = y.astype(BF16)


def _diff_proj(xs, mod, layer, norm_g, w_qkv, tabs, col0, ncols, rope, transpose, name):
    tn = 512
    const2 = lambda i, n: (0, 0)
    tab_spec = pl.BlockSpec((TM, 128), lambda i, n: (_rope_tile(i), 0))
    if transpose:
        out_spec = pl.BlockSpec((TM // TB, tn, TB), lambda i, n: (i, n, 0))
        out_shape = jax.ShapeDtypeStruct((T_ALL // TB, ncols, TB), BF16)
    else:
        out_spec = pl.BlockSpec((TM, tn), lambda i, n: (i, n))
        out_shape = jax.ShapeDtypeStruct((T_ALL, ncols), BF16)
    return pl.pallas_call(
        functools.partial(_proj_kernel, rope=rope, transpose=transpose),
        grid=(N_ALL_TILES, ncols // tn),
        in_specs=[
            pl.BlockSpec((TM, D), lambda i, n: (i, 0)),
            pl.BlockSpec((1, D), const2),
            _mod_spec(layer, 0), _mod_spec(layer, 1),
            pl.BlockSpec((D, tn), lambda i, n: (0, col0 // tn + n)),
            tab_spec, tab_spec, tab_spec,
        ],
        out_specs=out_spec,
        out_shape=out_shape,
        scratch_shapes=[pltpu.VMEM((TM, D), BF16)],
        compiler_params=_cparams(("arbitrary", "arbitrary")),
        name=f"{name}_l{layer}",
    )(xs, norm_g.reshape(1, D), mod, mod, w_qkv, *tabs)


TQ = 512
TK = 512
LOG2E = 1.4426950408889634


def _scores(k, q_t, c):
    return jnp.dot(k, q_t, preferred_element_type=F32) * c


def _attn_update(s, m_blk, v_t_parts, state, first):
    m_ref, l_ref, acc_ref = state
    m_new = m_blk if first else jnp.maximum(m_ref[...], m_blk)
    p = jnp.exp2(s - m_new)
    l_blk = jnp.sum(p, axis=0, keepdims=True)
    pb = p.astype(BF16)
    pv = None
    for t, v_t in enumerate(v_t_parts):
        d = jnp.dot(v_t, pb[t * TB:(t + 1) * TB, :], preferred_element_type=F32)
        pv = d if pv is None else pv + d
    if first:
        l_ref[...] = l_blk
        acc_ref[...] = pv
    else:
        alpha = jnp.exp2(m_ref[...] - m_new)
        l_ref[...] = alpha * l_ref[...] + l_blk
        acc_ref[...] = alpha * acc_ref[...] + pv
    m_ref[...] = m_new


def _attn_pipeline(score_fn, v_fn, states, slot_a, slot_b, n):
    def stash(j, slot):
        for s, (s_ref, mb_ref) in zip(score_fn(j), slot):
            s_ref[...] = s
            mb_ref[...] = jnp.max(s, axis=0, keepdims=True)

    def update(j, slot):
        v_parts = v_fn(j)
        for (s_ref, mb_ref), state in zip(slot, states):
            _attn_update(s_ref[...], mb_ref[...], v_parts, state, False)

    stash(0, slot_a)

    def pair(i, carry):
        j = 2 * i
        stash(j + 1, slot_b)
        update(j, slot_a)
        stash(j + 2, slot_a)
        update(j + 1, slot_b)
        return carry

    lax.fori_loop(0, n // 2 - 1, pair, 0)
    stash(n - 1, slot_b)
    update(n - 2, slot_a)
    update(n - 1, slot_b)


def _load_q_t(qt_ref, rows):
    return jnp.concatenate([qt_ref[t, rows, :] for t in range(qt_ref.shape[0])], axis=1)


def _lat_chunk_fns(k_ref, vt_ref):
    k_fn = lambda j: k_ref[pl.ds(pl.multiple_of(j * TK, TK), TK), :]
    v_fn = lambda j: [vt_ref[(TK // TB) * j + t] for t in range(TK // TB)]
    return k_fn, v_fn


def _slot_scratch(n_chain, tq):
    return [pltpu.VMEM((TK, tq), F32), pltpu.VMEM((1, tq), F32)] * (2 * n_chain)


def _split_slots(refs, n_chain):
    pairs = [(refs[2 * i], refs[2 * i + 1]) for i in range(2 * n_chain)]
    return pairs[:n_chain], pairs[n_chain:]


def _diff_attn_kernel(*refs, has_lat, scale, lambda_init):
    if has_lat:
        qt_ref, kc_ref, vtc_ref, k_ref, vt_ref, lam_ref, sg_ref, o_ref = refs[:8]
        scratch = refs[8:]
    else:
        qt_ref, kc_ref, vtc_ref, lam_ref, sg_ref, o_ref = refs[:6]
        scratch = refs[6:]
    states = (scratch[0:3], scratch[3:6])
    c = scale * LOG2E
    halves = (slice(0, DIFF_D), slice(DIFF_D, 2 * DIFF_D))

    def scores_of(k):
        return [_scores(k[:, halves[idx]], _load_q_t(qt_ref, halves[idx]), c) for idx in range(2)]

    for s, state in zip(scores_of(kc_ref[...]), states):
        _attn_update(s, jnp.max(s, axis=0, keepdims=True), [vtc_ref[0]], state, True)

    if has_lat:
        k_fn, v_fn = _lat_chunk_fns(k_ref, vt_ref)
        slot_a, slot_b = _split_slots(scratch[6:], 2)
        _attn_pipeline(lambda j: scores_of(k_fn(j)), v_fn, states, slot_a, slot_b, SEQ // TK)

    lam_v = lam_ref[...]
    lam = (jnp.exp(jnp.sum(lam_v[0:1] * lam_v[1:2], axis=-1, keepdims=True))
           - jnp.exp(jnp.sum(lam_v[2:3] * lam_v[3:4], axis=-1, keepdims=True)) + lambda_init)
    (_, l0, a0), (_, l1, a1) = states
    o_t = a0[...] * (1.0 / l0[...]) - lam * (a1[...] * (1.0 / l1[...]))
    o = o_t.T
    o = o * lax.rsqrt(jnp.mean(o * o, axis=-1, keepdims=True) + EPS) * sg_ref[...]
    o_ref[...] = (o * (1.0 - lambda_init)).astype(BF16)


def _diff_attn(q_t, k, v_t, lam_tab, subln_g, lambda_init, o_prev):
    has_lat = o_prev is None
    scale = DIFF_D ** -0.5
    ctx_blk0 = T_LAT // CTX
    if has_lat:
        tq = TQ
        grid = (BATCH, DIFF_H, SEQ // tq)
        q_blk = lambda b, h, i: b * (SEQ // tq) + i
    else:
        tq = CTX
        grid = (BATCH, DIFF_H, 1)
        q_blk = lambda b, h, i: ctx_blk0 + b
    in_specs = [
        pl.BlockSpec((tq // TB, DIFF_V, TB), lambda b, h, i: (q_blk(b, h, i), h, 0)),
        pl.BlockSpec((CTX, DIFF_V), lambda b, h, i: (ctx_blk0 + b, h)),
        pl.BlockSpec((1, DIFF_V, TB), lambda b, h, i: (ctx_blk0 + b, h, 0)),
    ]
    args = [q_t, k, v_t]
    if has_lat:
        in_specs += [
            pl.BlockSpec((SEQ, DIFF_V), lambda b, h, i: (b, h)),
            pl.BlockSpec((SEQ // TB, DIFF_V, TB), lambda b, h, i: (b, h, 0)),
        ]
        args += [k, v_t]
    in_specs += [
        pl.BlockSpec((8, DIFF_D), lambda b, h, i: (0, 0)),
        pl.BlockSpec((1, DIFF_V), lambda b, h, i: (0, 0)),
    ]
    args += [lam_tab, subln_g.reshape(1, DIFF_V)]
    aliases = {}
    kern = functools.partial(_diff_attn_kernel, has_lat=has_lat, scale=scale, lambda_init=lambda_init)
    if not has_lat:
        in_specs.append(pl.BlockSpec(memory_space=pl.ANY))
        args.append(o_prev)
        aliases = {len(args) - 1: 0}
        base = kern
        kern = lambda *refs: base(*refs[:5], *refs[6:])
    return pl.pallas_call(
        kern,
        grid=grid,
        in_specs=in_specs,
        out_specs=pl.BlockSpec((tq, DIFF_V), lambda b, h, i: (q_blk(b, h, i), h)),
        out_shape=jax.ShapeDtypeStruct((T_ALL, DIFF_H * DIFF_V), BF16),
        input_output_aliases=aliases,
        scratch_shapes=([pltpu.VMEM((1, tq), F32), pltpu.VMEM((1, tq), F32), pltpu.VMEM((DIFF_V, tq), F32)] * 2
                        + (_slot_scratch(2, tq) if has_lat else [])),
        compiler_params=_cparams(("arbitrary",) * 3, ATTN_FLAGS),
        name="diff_attn_lat" if has_lat else "diff_attn_ctx",
    )(*args)


def _mla_attn_kernel(qt_ref, kc_ref, vtc_ref, k_ref, vt_ref, o_ref, *scratch, scale):
    state = scratch[0:3]
    c = scale * LOG2E
    full = slice(0, MLA_QK_PAD)
    score = lambda k: _scores(k, _load_q_t(qt_ref, full), c)
    s = score(kc_ref[...])
    _attn_update(s, jnp.max(s, axis=0, keepdims=True), [vtc_ref[0]], state, True)
    k_fn, v_fn = _lat_chunk_fns(k_ref, vt_ref)
    slot_a, slot_b = _split_slots(scratch[3:], 1)
    _attn_pipeline(lambda j: [score(k_fn(j))], v_fn, (state,), slot_a, slot_b, SEQ // TK)
    o_ref[...] = (state[2][...] * (1.0 / state[1][...])).T.astype(BF16)


def _mla_attn(q_t, k, v_t):
    scale = MLA_QK ** -0.5
    ctx_blk0 = T_LAT // CTX
    return pl.pallas_call(
        functools.partial(_mla_attn_kernel, scale=scale),
        grid=(BATCH, MLA_H, SEQ // TQ),
        in_specs=[
            pl.BlockSpec((TQ // TB, MLA_QK_PAD, TB), lambda b, h, i: (b * (SEQ // TQ) + i, h, 0)),
            pl.BlockSpec((CTX, MLA_QK_PAD), lambda b, h, i: (ctx_blk0 + b, h)),
            pl.BlockSpec((1, MLA_V, TB), lambda b, h, i: (ctx_blk0 + b, h, 0)),
            pl.BlockSpec((SEQ, MLA_QK_PAD), lambda b, h, i: (b, h)),
            pl.BlockSpec((SEQ // TB, MLA_V, TB), lambda b, h, i: (b, h, 0)),
        ],
        out_specs=pl.BlockSpec((TQ, MLA_V), lambda b, h, i: (b * (SEQ // TQ) + i, h)),
        out_shape=jax.ShapeDtypeStruct((T_LAT, MLA_H * MLA_V), BF16),
        scratch_shapes=([pltpu.VMEM((1, TQ), F32), pltpu.VMEM((1, TQ), F32), pltpu.VMEM((MLA_V, TQ), F32)]
                        + _slot_scratch(1, TQ)),
        compiler_params=_cparams(("arbitrary",) * 3, ATTN_FLAGS),
        name="mla_attn",
    )(q_t, k, v_t, k, v_t)


def _mla_proj_kernel(x_ref, g_ref, sh_ref, sc_ref, wd_ref, qg_ref, kvg_ref, wuq_ref, wukv_ref,
                     cos_ref, sa_ref, sb_ref, qt_ref, k_ref, vt_ref):
    h = _norm_mod(x_ref[...], g_ref[...], sh_ref[...], sc_ref[...]).astype(BF16)
    c = jnp.dot(h, wd_ref[...], preferred_element_type=F32)
    cq = c[:, :MLA_QR_PAD]
    ckv = c[:, MLA_QR_PAD:MLA_QR_PAD + MLA_KVR]
    kr = c[:, MLA_QR_PAD + MLA_KVR:]
    cqn = cq * lax.rsqrt(jnp.sum(cq * cq, axis=-1, keepdims=True) * (1.0 / MLA_QR) + EPS) * qg_ref[...]
    ckvn = ckv * lax.rsqrt(jnp.mean(ckv * ckv, axis=-1, keepdims=True) + EPS) * kvg_ref[...]
    cos, sa, sb = cos_ref[...], sa_ref[...], sb_ref[...]
    kr_rot = _rope(kr, cos, sa, sb, 16).astype(BF16)
    q = jnp.dot(cqn.astype(BF16), wuq_ref[...], preferred_element_type=F32)
    kv = jnp.dot(ckvn.astype(BF16), wukv_ref[...], preferred_element_type=F32)
    for hd in range(MLA_H):
        base = hd * MLA_QK_PAD
        qt_ref[0, base:base + 128, :] = q[:, base:base + 128].T.astype(BF16)
        qt_ref[0, base + 128:base + 256, :] = _rope(q[:, base + 128:base + 256], cos, sa, sb, 16).T.astype(BF16)
        k_ref[:, base:base + 128] = kv[:, hd * MLA_NOPE:(hd + 1) * MLA_NOPE].astype(BF16)
        k_ref[:, base + 128:base + 256] = kr_rot
        vcols = slice(MLA_H * MLA_NOPE + hd * MLA_V, MLA_H * MLA_NOPE + (hd + 1) * MLA_V)
        vt_ref[0, hd * MLA_V:(hd + 1) * MLA_V, :] = kv[:, vcols].T.astype(BF16)


def _mla_proj(xs, mod, layer, norm_g, w_d, q_g, kv_g, w_uq, w_ukv, tabs):
    tm = TB
    scale = TM // tm
    const2 = lambda i: (0, 0)
    mod_spec = lambda k: pl.BlockSpec((None, None, 1, D), lambda i: (layer, _mod_row(i // scale), 0, k))
    tab_spec = pl.BlockSpec((tm, 128), lambda i: (
        jnp.where(i < N_LAT_TILES * scale, i % (TILES_PER_BATCH * scale), TILES_PER_BATCH * scale), 0))
    n_q = MLA_H * MLA_QK_PAD
    return pl.pallas_call(
        _mla_proj_kernel,
        grid=(N_ALL_TILES * scale,),
        in_specs=[
            pl.BlockSpec((tm, D), lambda i: (i, 0)),
            pl.BlockSpec((1, D), const2),
            mod_spec(0), mod_spec(1),
            pl.BlockSpec((D, MLA_C_PAD), const2),
            pl.BlockSpec((1, MLA_QR_PAD), const2),
            pl.BlockSpec((1, MLA_KVR), const2),
            pl.BlockSpec((MLA_QR_PAD, n_q), const2),
            pl.BlockSpec((MLA_KVR, MLA_H * (MLA_NOPE + MLA_V)), const2),
            tab_spec, tab_spec, tab_spec,
        ],
        out_specs=[
            pl.BlockSpec((1, n_q, tm), lambda i: (i, 0, 0)),
            pl.BlockSpec((tm, n_q), lambda i: (i, 0)),
            pl.BlockSpec((1, MLA_H * MLA_V, tm), lambda i: (i, 0, 0)),
        ],
        out_shape=[
            jax.ShapeDtypeStruct((T_ALL // tm, n_q, tm), BF16),
            jax.ShapeDtypeStruct((T_ALL, n_q), BF16),
            jax.ShapeDtypeStruct((T_ALL // tm, MLA_H * MLA_V, tm), BF16),
        ],
        compiler_params=_cparams(("arbitrary",)),
        name=f"mla_proj_l{layer}",
    )(xs, norm_g.reshape(1, D), mod, mod, w_d, q_g, kv_g, w_uq, w_ukv, *tabs)


def _rope_tables(rot_dim):
    a = rot_dim // 2
    ch = a // 2
    pos = jnp.arange(SEQ, dtype=jnp.int32)
    rows = (pos // GRID_W).astype(F32)
    cols = (pos % GRID_W).astype(F32)
    inv = ROPE_BASE ** (-jnp.arange(0, a, 2, dtype=F32) / a)

    def axis_angles(p):
        ang = p[:, None] * inv[None, :]
        return jnp.concatenate([ang, ang], axis=-1)

    ang = jnp.concatenate([axis_angles(rows), axis_angles(cols)], axis=-1)
    cos, sin = jnp.cos(ang), jnp.sin(ang)
    lane = jnp.arange(rot_dim)
    first = (lane % a) < ch
    sin_a = jnp.where(first[None, :], -sin, 0.0)
    sin_b = jnp.where(first[None, :], 0.0, sin)
    pad = 128 - rot_dim
    cos = jnp.pad(cos, ((0, TM), (0, pad)), constant_values=1.0)
    cos = cos.at[SEQ:, :].set(1.0)
    sin_a = jnp.pad(sin_a, ((0, TM), (0, pad)))
    sin_b = jnp.pad(sin_b, ((0, TM), (0, pad)))
    return cos, sin_a, sin_b


def kernel(x, c, ctx, c_ctx, w_mod, b_mod, norm1_g, norm2_g, w_ff1, w_ff2, a_w_in, a_b_in, a_ln_g, a_w_s, a_b_s, a_w_out, b_w_qkv, b_lam_q1, b_lam_k1, b_lam_q2, b_lam_k2, b_subln_g, b_w_o, c_w_dqkv, c_q_norm_g, c_w_uq, c_kv_norm_g, c_w_ukv, c_w_o, final_g):
    xs = jnp.concatenate([x.reshape(T_LAT, D), ctx.reshape(T_CTX, D)], axis=0)
    cc = jnp.concatenate([c, c_ctx[None, :], jnp.zeros((8 - BATCH - 1, D), F32)], axis=0)
    mod = _modulation(cc, w_mod, b_mod).reshape(DEPTH, 8, 1, N_MOD * D)

    tabs_b = _rope_tables(DIFF_D)
    tabs_c = _rope_tables(MLA_ROPE)

    out = None
    for l in range(DEPTH):
        kind, idx = l % 3, l // 3
        ctx_later = any(j % 3 != 0 for j in range(l + 1, DEPTH))
        last = l == DEPTH - 1
        n_tiles = N_ALL_TILES if ctx_later else N_LAT_TILES
        if kind == 0:
            b_s = jnp.broadcast_to(a_b_s[idx][:, :, None], (GROUPS, CHUNK, CHUNK))
            t = _gmlp_mix(xs, mod, l, n_tiles, norm1_g[l], a_w_in[idx].astype(BF16), a_b_in[idx],
                          a_ln_g[idx], a_w_s[idx].astype(BF16), b_s)
            xs = _out_proj(xs, t, a_w_out[idx].astype(BF16), mod, l, n_tiles)
        elif kind == 1:
            lambda_init = 0.8 - 0.6 * math.exp(-0.3 * l)
            w_qkv = b_w_qkv[idx].astype(BF16)
            nqk = 2 * DIFF_H * DIFF_D
            q_t = _diff_proj(xs, mod, l, norm1_g[l], w_qkv, tabs_b, 0, nqk, True, True, "diff_q")
            k = _diff_proj(xs, mod, l, norm1_g[l], w_qkv, tabs_b, nqk, nqk, True, False, "diff_k")
            v_t = _diff_proj(xs, mod, l, norm1_g[l], w_qkv, tabs_b, 2 * nqk, DIFF_H * DIFF_V, False, True,
                             "diff_v")
            lam_tab = jnp.concatenate([b_lam_q1[idx][None], b_lam_k1[idx][None], b_lam_q2[idx][None],
                                       b_lam_k2[idx][None], jnp.zeros((4, DIFF_D), F32)], axis=0)
            o = _diff_attn(q_t, k, v_t, lam_tab, b_subln_g[idx], lambda_init, None)
            if ctx_later:
                o = _diff_attn(q_t, k, v_t, lam_tab, b_subln_g[idx], lambda_init, o)
            xs = _out_proj(xs, o, b_w_o[idx].astype(BF16), mod, l, n_tiles)
        else:
            w_d = c_w_dqkv[idx]
            w_d = jnp.concatenate([
                jnp.pad(w_d[:, :MLA_QR], ((0, 0), (0, MLA_QR_PAD - MLA_QR))),
                w_d[:, MLA_QR:MLA_QR + MLA_KVR],
                jnp.pad(w_d[:, MLA_QR + MLA_KVR:], ((0, 0), (0, 128 - MLA_ROPE)))], axis=1).astype(BF16)
            w_uq = jnp.pad(c_w_uq[idx].reshape(MLA_QR, MLA_H, MLA_QK),
                           ((0, MLA_QR_PAD - MLA_QR), (0, 0), (0, MLA_QK_PAD - MLA_QK)))
            w_uq = w_uq.reshape(MLA_QR_PAD, MLA_H * MLA_QK_PAD).astype(BF16)
            w_ukv = c_w_ukv[idx].reshape(MLA_KVR, MLA_H, MLA_NOPE + MLA_V)
            w_ukv = jnp.concatenate([w_ukv[:, :, :MLA_NOPE].reshape(MLA_KVR, MLA_H * MLA_NOPE),
                                     w_ukv[:, :, MLA_NOPE:].reshape(MLA_KVR, MLA_H * MLA_V)],
                                    axis=1).astype(BF16)
            q_g = jnp.pad(c_q_norm_g[idx], (0, MLA_QR_PAD - MLA_QR)).reshape(1, MLA_QR_PAD)
            q_t, k, v_t = _mla_proj(xs, mod, l, norm1_g[l], w_d, q_g, c_kv_norm_g[idx].reshape(1, MLA_KVR),
                                    w_uq, w_ukv, tabs_c)
            o = _mla_attn(q_t, k, v_t)
            xs = _out_proj(xs, o, c_w_o[idx].astype(BF16), mod, l, N_LAT_TILES)
        if ctx_later and kind == 2:
            raise NotImplementedError("context update after an MLA layer is not needed at this depth")
        res = _ffn(xs, mod, l, n_tiles, norm2_g[l], w_ff1[l].astype(BF16), w_ff2[l].astype(BF16),
                   final_g, last)
        if last:
            out = res
        else:
            xs = res
    return out.reshape(BATCH, SEQ, D)
```

```python
import functools
import math

import jax
import jax.numpy as jnp
from jax import lax
from jax.experimental import pallas as pl
from jax.experimental.pallas import tpu as pltpu

D = 2048
BATCH = 2
SEQ = 8192
DEPTH = 4
GRID_W = 64
CTX = 256
N_MOD = 6
D_FF = 4 * D
EPS = 1e-6
ROPE_BASE = 10000.0

CHUNK = 128
GROUPS = 16
GDIM = 128

DIFF_H = 8
DIFF_D = 128
DIFF_V = 256

MLA_H = 16
MLA_QR = 448
MLA_QR_PAD = 512
MLA_KVR = 512
MLA_NOPE = 128
MLA_ROPE = 64
MLA_V = 128
MLA_QK = MLA_NOPE + MLA_ROPE
MLA_QK_PAD = 256
MLA_C_PAD = MLA_QR_PAD + MLA_KVR + 128

T_LAT = BATCH * SEQ
T_CTX = BATCH * CTX
T_ALL = T_LAT + T_CTX

TM = 512
TILES_PER_BATCH = SEQ // TM
N_LAT_TILES = T_LAT // TM
N_ALL_TILES = T_ALL // TM
ROPE_ID_TILE = SEQ // TM

VMEM_LIMIT = 56 * 1024 * 1024

BF16 = jnp.bfloat16
F32 = jnp.float32


def _cparams(sem, flags=None):
    return pltpu.CompilerParams(dimension_semantics=sem, vmem_limit_bytes=VMEM_LIMIT, flags=flags)


ATTN_FLAGS = None


def _mod_row(i):
    return jnp.minimum(i // TILES_PER_BATCH, 2)


def _mod_spec(layer, k):
    return pl.BlockSpec((None, None, 1, D), lambda i, *_: (layer, _mod_row(i), 0, k))


def _rope_tile(i):
    return jnp.where(i < N_LAT_TILES, i % TILES_PER_BATCH, ROPE_ID_TILE)


def _norm_mod(x, g, sh, sc):
    y = x * lax.rsqrt(jnp.mean(x * x, axis=-1, keepdims=True) + EPS)
    y = y * g
    return y * (1.0 + sc) + sh


def _rope(x, cos, sin_a, sin_b, ch):
    return (x * cos + pltpu.roll(x, 128 - ch, 1) * sin_a + pltpu.roll(x, ch, 1) * sin_b)


def _mod_kernel(c_ref, w_ref, b_ref, o_ref):
    c = c_ref[...]
    s = (c * jax.nn.sigmoid(c)).astype(BF16)
    o_ref[...] = jnp.dot(s, w_ref[...].astype(BF16), preferred_element_type=F32) + b_ref[...]


def _modulation(cc, w_mod, b_mod):
    tn = 1024
    n = N_MOD * D
    return pl.pallas_call(
        _mod_kernel,
        grid=(DEPTH, n // tn),
        in_specs=[
            pl.BlockSpec((8, D), lambda l, j: (0, 0)),
            pl.BlockSpec((None, D, tn), lambda l, j: (l, 0, j)),
            pl.BlockSpec((None, 1, tn), lambda l, j: (l, 0, j)),
        ],
        out_specs=pl.BlockSpec((None, 8, tn), lambda l, j: (l, 0, j)),
        out_shape=jax.ShapeDtypeStruct((DEPTH, 8, n), F32),
        compiler_params=_cparams(("arbitrary", "arbitrary")),
        name="modulation",
    )(cc, w_mod, b_mod.reshape(DEPTH, 1, n))


def _gelu_tanh(x):
    return 0.5 * x * (1.0 + jnp.tanh(math.sqrt(2.0 / math.pi) * (x + 0.044715 * (x * x * x))))


def _gmlp_kernel(x_ref, g_ref, sh_ref, sc_ref, win_ref, bin_ref, lng_ref, ws_ref, bs_ref, o_ref,
                 vn_ref):
    tm = x_ref.shape[0]
    nchunk = tm // CHUNK
    h = _norm_mod(x_ref[...], g_ref[...], sh_ref[...], sc_ref[...]).astype(BF16)
    zv = _gelu_tanh(jnp.dot(h, win_ref[:, D:], preferred_element_type=F32) + bin_ref[:, D:])
    mu = jnp.mean(zv, axis=-1, keepdims=True)
    zc = zv - mu
    vn = zc * lax.rsqrt(jnp.mean(zc * zc, axis=-1, keepdims=True) + EPS) * lng_ref[...]
    vn_ref[...] = vn.astype(BF16)
    ublk = 512
    for ub in range(D // ublk):
        zu = _gelu_tanh(jnp.dot(h, win_ref[:, ub * ublk:(ub + 1) * ublk], preferred_element_type=F32)
                        + bin_ref[:, ub * ublk:(ub + 1) * ublk])
        for gg in range(ublk // GDIM):
            grp = ub * (ublk // GDIM) + gg
            cols = slice(grp * GDIM, (grp + 1) * GDIM)
            rhs = jnp.concatenate([vn_ref[c * CHUNK:(c + 1) * CHUNK, cols] for c in range(nchunk)], axis=1)
            sv = jnp.dot(ws_ref[grp], rhs, preferred_element_type=F32)
            for c in range(nchunk):
                svc = sv[:, c * CHUNK:(c + 1) * CHUNK] + bs_ref[grp]
                u = zu[c * CHUNK:(c + 1) * CHUNK, gg * GDIM:(gg + 1) * GDIM]
                o_ref[c * CHUNK:(c + 1) * CHUNK, cols] = (u * svc).astype(BF16)


def _gmlp_mix(xs, mod, layer, n_tiles, norm_g, w_in, b_in, ln_g, w_s, b_s):
    tm = 256
    scale = TM // tm
    const2 = lambda i: (0, 0)
    mod_spec = lambda k: pl.BlockSpec((None, None, 1, D), lambda i: (layer, _mod_row(i // scale), 0, k))
    return pl.pallas_call(
        _gmlp_kernel,
        grid=(n_tiles * scale,),
        in_specs=[
            pl.BlockSpec((tm, D), lambda i: (i, 0)),
            pl.BlockSpec((1, D), const2),
            mod_spec(0), mod_spec(1),
            pl.BlockSpec((D, 2 * D), const2),
            pl.BlockSpec((1, 2 * D), const2),
            pl.BlockSpec((1, D), const2),
            pl.BlockSpec((GROUPS, CHUNK, CHUNK), lambda i: (0, 0, 0)),
            pl.BlockSpec((GROUPS, CHUNK, CHUNK), lambda i: (0, 0, 0)),
        ],
        out_specs=pl.BlockSpec((tm, D), lambda i: (i, 0)),
        out_shape=jax.ShapeDtypeStruct((T_ALL, D), BF16),
        scratch_shapes=[pltpu.VMEM((tm, D), BF16)],
        compiler_params=_cparams(("arbitrary",)),
        name=f"gmlp_mix_l{layer}",
    )(xs, norm_g.reshape(1, D), mod, mod, w_in, b_in.reshape(1, 2 * D), ln_g.reshape(1, D), w_s, b_s)


def _out_proj_kernel(x_ref, a_ref, w_ref, gate_ref, o_ref):
    y = jnp.dot(a_ref[...], w_ref[...], preferred_element_type=F32)
    o_ref[...] = x_ref[...] + gate_ref[...] * y


def _out_proj(xs, a, w, mod, layer, n_tiles):
    ka = a.shape[1]
    return pl.pallas_call(
        _out_proj_kernel,
        grid=(n_tiles,),
        in_specs=[
            pl.BlockSpec((TM, D), lambda i: (i, 0)),
            pl.BlockSpec((TM, ka), lambda i: (i, 0)),
            pl.BlockSpec((ka, D), lambda i: (0, 0)),
            _mod_spec(layer, 2),
        ],
        out_specs=pl.BlockSpec((TM, D), lambda i: (i, 0)),
        out_shape=jax.ShapeDtypeStruct((T_ALL, D), F32),
        input_output_aliases={0: 0},
        compiler_params=_cparams(("arbitrary",)),
        name=f"out_proj_l{layer}",
    )(xs, a, w, mod)


def _ffn_kernel(x_ref, g_ref, sh_ref, sc_ref, gate_ref, w1_ref, w2_ref, fg_ref, o_ref, h_ref, *,
                n_f, final_norm):
    f = pl.program_id(1)

    @pl.when(f == 0)
    def _():
        h_ref[...] = _norm_mod(x_ref[...], g_ref[...], sh_ref[...], sc_ref[...]).astype(BF16)
        o_ref[...] = jnp.zeros_like(o_ref)

    a = jnp.dot(h_ref[...], w1_ref[...], preferred_element_type=F32)
    a = jnp.square(jnp.maximum(a, 0.0)).astype(BF16)
    o_ref[...] += jnp.dot(a, w2_ref[...], preferred_element_type=F32)

    @pl.when(f == n_f - 1)
    def _():
        y = x_ref[...] + gate_ref[...] * o_ref[...]
        if final_norm:
            y = y * lax.rsqrt(jnp.mean(y * y, axis=-1, keepdims=True) + EPS) * fg_ref[...]
        o_ref[...] = y


def _ffn(xs, mod, layer, n_tiles, norm_g, w1, w2, final_g, final_norm):
    tf = 1024
    n_f = D_FF // tf
    rows = T_LAT if final_norm else T_ALL
    const2 = lambda i, f: (0, 0)
    return pl.pallas_call(
        functools.partial(_ffn_kernel, n_f=n_f, final_norm=final_norm),
        grid=(n_tiles, n_f),
        in_specs=[
            pl.BlockSpec((TM, D), lambda i, f: (i, 0)),
            pl.BlockSpec((1, D), const2),
            _mod_spec(layer, 3), _mod_spec(layer, 4), _mod_spec(layer, 5),
            pl.BlockSpec((None, D, tf), lambda i, f: (layer, 0, f)),
            pl.BlockSpec((None, tf, D), lambda i, f: (layer, f, 0)),
            pl.BlockSpec((1, D), const2),
        ],
        out_specs=pl.BlockSpec((TM, D), lambda i, f: (i, 0)),
        out_shape=jax.ShapeDtypeStruct((rows, D), F32),
        input_output_aliases={} if final_norm else {0: 0},
        scratch_shapes=[pltpu.VMEM((TM, D), BF16)],
        compiler_params=_cparams(("arbitrary", "arbitrary")),
        name=f"ffn_l{layer}",
    )(xs, norm_g.reshape(1, D), mod, mod, mod, w1, w2, final_g.reshape(1, D))


TB = 256
LOG2E = 1.4426950408889634
DIFF_Q_SCALE = DIFF_D ** -0.5 * LOG2E
MLA_Q_SCALE = MLA_QK ** -0.5 * LOG2E
DIFF_BAND = DIFF_H * DIFF_V
PROJ_SUB = 512


def _diff_proj_kernel(x_ref, g_ref, sh_ref, sc_ref, w_ref, cos_ref, sa_ref, sb_ref,
                      qt_ref, k_ref, vt_ref, h_ref):
    n = pl.program_id(1)

    @pl.when(n == 0)
    def _():
        h_ref[...] = _norm_mod(x_ref[...], g_ref[...], sh_ref[...], sc_ref[...]).astype(BF16)

    def sub_dot(s):
        return jnp.dot(h_ref[...], w_ref[:, s * PROJ_SUB:(s + 1) * PROJ_SUB], preferred_element_type=F32)

    def roped(y):
        cos, sa, sb = cos_ref[...], sa_ref[...], sb_ref[...]
        return jnp.concatenate([_rope(y[:, c * 128:(c + 1) * 128], cos, sa, sb, 32)
                                for c in range(PROJ_SUB // 128)], axis=1)

    def store_t(ref, s, y):
        for t in range(TM // TB):
            ref[t, s * PROJ_SUB:(s + 1) * PROJ_SUB, :] = y[t * TB:(t + 1) * TB, :].T.astype(BF16)

    @pl.when(n == 0)
    def _():
        for s in range(DIFF_BAND // PROJ_SUB):
            store_t(qt_ref, s, roped(sub_dot(s) * DIFF_Q_SCALE))

    @pl.when(n == 1)
    def _():
        for s in range(DIFF_BAND // PROJ_SUB):
            k_ref[:, s * PROJ_SUB:(s + 1) * PROJ_SUB] = roped(sub_dot(s)).astype(BF16)

    @pl.when(n == 2)
    def _():
        for s in range(DIFF_BAND // PROJ_SUB):
            store_t(vt_ref, s, sub_dot(s))


def _diff_proj(xs, mod, layer, norm_g, w_qkv, tabs):
    const2 = lambda i, n: (0, 0)
    tab_spec = pl.BlockSpec((TM, 128), lambda i, n: (_rope_tile(i), 0))
    t_spec = pl.BlockSpec((TM // TB, DIFF_BAND, TB), lambda i, n: (i, 0, 0))
    t_shape = jax.ShapeDtypeStruct((T_ALL // TB, DIFF_BAND, TB), BF16)
    return pl.pallas_call(
        _diff_proj_kernel,
        grid=(N_ALL_TILES, 3),
        in_specs=[
            pl.BlockSpec((TM, D), lambda i, n: (i, 0)),
            pl.BlockSpec((1, D), const2),
            _mod_spec(layer, 0), _mod_spec(layer, 1),
            pl.BlockSpec((D, DIFF_BAND), lambda i, n: (0, n)),
            tab_spec, tab_spec, tab_spec,
        ],
        out_specs=[t_spec, pl.BlockSpec((TM, DIFF_BAND), lambda i, n: (i, 0)), t_spec],
        out_shape=[t_shape, jax.ShapeDtypeStruct((T_ALL, DIFF_BAND), BF16), t_shape],
        scratch_shapes=[pltpu.VMEM((TM, D), BF16)],
        compiler_params=_cparams(("arbitrary", "arbitrary")),
        name=f"diff_proj_l{layer}",
    )(xs, norm_g.reshape(1, D), mod, mod, w_qkv, *tabs)


TQ = 512
TK = 512
SUM_ROWS = 16


def _scores(k, q_t):
    return jnp.dot(k, q_t, preferred_element_type=F32)


def _attn_update(s, m_blk, v_t_parts, state, first):
    m_ref, l_ref, acc_ref = state
    dv = acc_ref.shape[0]
    m_new = m_blk if first else jnp.maximum(m_ref[...], m_blk)
    pb = jnp.exp2(s - m_new).astype(BF16)
    ones = jnp.ones((SUM_ROWS, TB), BF16)
    pv = None
    for t, v_t in enumerate(v_t_parts):
        d = jnp.dot(jnp.concatenate([v_t, ones], axis=0), pb[t * TB:(t + 1) * TB, :],
                    preferred_element_type=F32)
        pv = d if pv is None else pv + d
    l_blk = pv[dv:dv + 1, :]
    if first:
        l_ref[...] = l_blk
        acc_ref[...] = pv[:dv, :]
    else:
        alpha = jnp.exp2(m_ref[...] - m_new)
        l_ref[...] = alpha * l_ref[...] + l_blk
        acc_ref[...] = alpha * acc_ref[...] + pv[:dv, :]
    m_ref[...] = m_new


def _attn_pipeline(score_fn, v_fn, head_scores, head_v, states, slot_a, slot_b, n):
    def stash(j, slot):
        for s, (s_ref, mb_ref) in zip(score_fn(j), slot):
            s_ref[...] = s
            mb_ref[...] = jnp.max(s, axis=0, keepdims=True)

    def update(j, slot):
        for (s_ref, mb_ref), v_parts, state in zip(slot, v_fn(j), states):
            _attn_update(s_ref[...], mb_ref[...], v_parts, state, False)

    stash(0, slot_a)
    for s, v_parts, state in zip(head_scores, head_v, states):
        _attn_update(s, jnp.max(s, axis=0, keepdims=True), v_parts, state, True)

    def pair(i, carry):
        j = 2 * i
        stash(j + 1, slot_b)
        update(j, slot_a)
        stash(j + 2, slot_a)
        update(j + 1, slot_b)
        return carry

    lax.fori_loop(0, n // 2 - 1, pair, 0)
    stash(n - 1, slot_b)
    update(n - 2, slot_a)
    update(n - 1, slot_b)


def _load_q_t(qt_ref, rows):
    return jnp.concatenate([qt_ref[t, rows, :] for t in range(qt_ref.shape[0])], axis=1)


def _lat_chunk_fns(k_ref, vt_ref, v_rows):
    k_fn = lambda j: k_ref[pl.ds(pl.multiple_of(j * TK, TK), TK), :]
    v_fn = lambda j: [[vt_ref[(TK // TB) * j + t, rows, :] for t in range(TK // TB)] for rows in v_rows]
    return k_fn, v_fn


def _slot_scratch(n_chain, tq):
    return [pltpu.VMEM((TK, tq), F32), pltpu.VMEM((1, tq), F32)] * (2 * n_chain)


def _split_slots(refs, n_chain):
    pairs = [(refs[2 * i], refs[2 * i + 1]) for i in range(2 * n_chain)]
    return pairs[:n_chain], pairs[n_chain:]


def _diff_attn_kernel(*refs, has_lat, lambda_init):
    if has_lat:
        qt_ref, kc_ref, vtc_ref, k_ref, vt_ref, lam_ref, sg_ref, o_ref = refs[:8]
        scratch = refs[8:]
    else:
        qt_ref, kc_ref, vtc_ref, lam_ref, sg_ref, o_ref = refs[:6]
        scratch = refs[6:]
    states = (scratch[0:3], scratch[3:6])
    halves = (slice(0, DIFF_D), slice(DIFF_D, 2 * DIFF_D))

    def scores_of(k):
        return [_scores(k[:, halves[idx]], _load_q_t(qt_ref, halves[idx])) for idx in range(2)]

    ctx_scores = scores_of(kc_ref[...])
    v_rows = [slice(0, DIFF_V)] * 2
    ctx_v = [[vtc_ref[0]]] * 2
    if has_lat:
        k_fn, v_fn = _lat_chunk_fns(k_ref, vt_ref, v_rows)
        slot_a, slot_b = _split_slots(scratch[6:], 2)
        _attn_pipeline(lambda j: scores_of(k_fn(j)), v_fn, ctx_scores, ctx_v, states,
                       slot_a, slot_b, SEQ // TK)
    else:
        for s, v_parts, state in zip(ctx_scores, ctx_v, states):
            _attn_update(s, jnp.max(s, axis=0, keepdims=True), v_parts, state, True)

    lam_v = lam_ref[...]
    lam = (jnp.exp(jnp.sum(lam_v[0:1] * lam_v[1:2], axis=-1, keepdims=True))
           - jnp.exp(jnp.sum(lam_v[2:3] * lam_v[3:4], axis=-1, keepdims=True)) + lambda_init)
    (_, l0, a0), (_, l1, a1) = states
    o_t = a0[...] * (1.0 / l0[...]) - lam * (a1[...] * (1.0 / l1[...]))
    o = o_t.T
    o = o * lax.rsqrt(jnp.mean(o * o, axis=-1, keepdims=True) + EPS) * sg_ref[...]
    o_ref[...] = (o * (1.0 - lambda_init)).astype(BF16)


def _diff_attn(q_t, k, v_t, lam_tab, subln_g, lambda_init, o_prev):
    has_lat = o_prev is None
    ctx_blk0 = T_LAT // CTX
    if has_lat:
        tq = TQ
        grid = (BATCH, DIFF_H, SEQ // tq)
        q_blk = lambda b, h, i: b * (SEQ // tq) + i
    else:
        tq = CTX
        grid = (BATCH, DIFF_H, 1)
        q_blk = lambda b, h, i: ctx_blk0 + b
    in_specs = [
        pl.BlockSpec((tq // TB, DIFF_V, TB), lambda b, h, i: (q_blk(b, h, i), h, 0)),
        pl.BlockSpec((CTX, DIFF_V), lambda b, h, i: (ctx_blk0 + b, h)),
        pl.BlockSpec((1, DIFF_V, TB), lambda b, h, i: (ctx_blk0 + b, h, 0)),
    ]
    args = [q_t, k, v_t]
    if has_lat:
        in_specs += [
            pl.BlockSpec((SEQ, DIFF_V), lambda b, h, i: (b, h)),
            pl.BlockSpec((SEQ // TB, DIFF_V, TB), lambda b, h, i: (b, h, 0)),
        ]
        args += [k, v_t]
    in_specs += [
        pl.BlockSpec((8, DIFF_D), lambda b, h, i: (0, 0)),
        pl.BlockSpec((1, DIFF_V), lambda b, h, i: (0, 0)),
    ]
    args += [lam_tab, subln_g.reshape(1, DIFF_V)]
    aliases = {}
    kern = functools.partial(_diff_attn_kernel, has_lat=has_lat, lambda_init=lambda_init)
    if not has_lat:
        in_specs.append(pl.BlockSpec(memory_space=pl.ANY))
        args.append(o_prev)
        aliases = {len(args) - 1: 0}
        base = kern
        kern = lambda *refs: base(*refs[:5], *refs[6:])
    return pl.pallas_call(
        kern,
        grid=grid,
        in_specs=in_specs,
        out_specs=pl.BlockSpec((tq, DIFF_V), lambda b, h, i: (q_blk(b, h, i), h)),
        out_shape=jax.ShapeDtypeStruct((T_ALL, DIFF_H * DIFF_V), BF16),
        input_output_aliases=aliases,
        scratch_shapes=([pltpu.VMEM((1, tq), F32), pltpu.VMEM((1, tq), F32), pltpu.VMEM((DIFF_V, tq), F32)] * 2
                        + (_slot_scratch(2, tq) if has_lat else [])),
        compiler_params=_cparams(("arbitrary",) * 3, ATTN_FLAGS),
        name="diff_attn_lat" if has_lat else "diff_attn_ctx",
    )(*args)


MLA_HPS = 2


def _mla_attn_kernel(qt_ref, kc_ref, vtc_ref, k_ref, vt_ref, o_ref, *scratch):
    states = [scratch[3 * i:3 * i + 3] for i in range(MLA_HPS)]
    qk = [slice(i * MLA_QK_PAD, (i + 1) * MLA_QK_PAD) for i in range(MLA_HPS)]
    v_rows = [slice(i * MLA_V, (i + 1) * MLA_V) for i in range(MLA_HPS)]

    def scores_of(k):
        return [_scores(k[:, qk[i]], _load_q_t(qt_ref, qk[i])) for i in range(MLA_HPS)]

    k_fn, v_fn = _lat_chunk_fns(k_ref, vt_ref, v_rows)
    slot_a, slot_b = _split_slots(scratch[3 * MLA_HPS:], MLA_HPS)
    _attn_pipeline(lambda j: scores_of(k_fn(j)), v_fn, scores_of(kc_ref[...]),
                   [[vtc_ref[0, rows, :]] for rows in v_rows], states, slot_a, slot_b, SEQ // TK)
    for i, (_, l_ref, acc_ref) in enumerate(states):
        o_ref[:, v_rows[i]] = (acc_ref[...] * (1.0 / l_ref[...])).T.astype(BF16)


def _mla_attn(q_t, k, v_t):
    ctx_blk0 = T_LAT // CTX
    qk_w, v_w = MLA_HPS * MLA_QK_PAD, MLA_HPS * MLA_V
    return pl.pallas_call(
        _mla_attn_kernel,
        grid=(BATCH, MLA_H // MLA_HPS, SEQ // TQ),
        in_specs=[
            pl.BlockSpec((TQ // TB, qk_w, TB), lambda b, h, i: (b * (SEQ // TQ) + i, h, 0)),
            pl.BlockSpec((CTX, qk_w), lambda b, h, i: (ctx_blk0 + b, h)),
            pl.BlockSpec((1, v_w, TB), lambda b, h, i: (ctx_blk0 + b, h, 0)),
            pl.BlockSpec((SEQ, qk_w), lambda b, h, i: (b, h)),
            pl.BlockSpec((SEQ // TB, v_w, TB), lambda b, h, i: (b, h, 0)),
        ],
        out_specs=pl.BlockSpec((TQ, v_w), lambda b, h, i: (b * (SEQ // TQ) + i, h)),
        out_shape=jax.ShapeDtypeStruct((T_LAT, MLA_H * MLA_V), BF16),
        scratch_shapes=([pltpu.VMEM((1, TQ), F32), pltpu.VMEM((1, TQ), F32), pltpu.VMEM((MLA_V, TQ), F32)]
                        * MLA_HPS + _slot_scratch(MLA_HPS, TQ)),
        compiler_params=_cparams(("arbitrary",) * 3, ATTN_FLAGS),
        name="mla_attn",
    )(q_t, k, v_t, k, v_t)


def _mla_proj_kernel(x_ref, g_ref, sh_ref, sc_ref, wd_ref, qg_ref, kvg_ref, wuq_ref, wukv_ref,
                     cos_ref, sa_ref, sb_ref, qt_ref, k_ref, vt_ref):
    h = _norm_mod(x_ref[...], g_ref[...], sh_ref[...], sc_ref[...]).astype(BF16)
    c = jnp.dot(h, wd_ref[...], preferred_element_type=F32)
    cq = c[:, :MLA_QR_PAD]
    ckv = c[:, MLA_QR_PAD:MLA_QR_PAD + MLA_KVR]
    kr = c[:, MLA_QR_PAD + MLA_KVR:]
    cqn = cq * lax.rsqrt(jnp.sum(cq * cq, axis=-1, keepdims=True) * (1.0 / MLA_QR) + EPS) * qg_ref[...]
    ckvn = ckv * lax.rsqrt(jnp.mean(ckv * ckv, axis=-1, keepdims=True) + EPS) * kvg_ref[...]
    cos, sa, sb = cos_ref[...], sa_ref[...], sb_ref[...]
    kr_rot = _rope(kr, cos, sa, sb, 16).astype(BF16)
    q = jnp.dot(cqn.astype(BF16), wuq_ref[...], preferred_element_type=F32) * MLA_Q_SCALE
    kv = jnp.dot(ckvn.astype(BF16), wukv_ref[...], preferred_element_type=F32)
    for hd in range(MLA_H):
        base = hd * MLA_QK_PAD
        qt_ref[0, base:base + 128, :] = q[:, base:base + 128].T.astype(BF16)
        qt_ref[0, base + 128:base + 256, :] = _rope(q[:, base + 128:base + 256], cos, sa, sb, 16).T.astype(BF16)
        k_ref[:, base:base + 128] = kv[:, hd * MLA_NOPE:(hd + 1) * MLA_NOPE].astype(BF16)
        k_ref[:, base + 128:base + 256] = kr_rot
        vcols = slice(MLA_H * MLA_NOPE + hd * MLA_V, MLA_H * MLA_NOPE + (hd + 1) * MLA_V)
        vt_ref[0, hd * MLA_V:(hd + 1) * MLA_V, :] = kv[:, vcols].T.astype(BF16)


def _mla_proj(xs, mod, layer, norm_g, w_d, q_g, kv_g, w_uq, w_ukv, tabs):
    tm = TB
    scale = TM // tm
    const2 = lambda i: (0, 0)
    mod_spec = lambda k: pl.BlockSpec((None, None, 1, D), lambda i: (layer, _mod_row(i // scale), 0, k))
    tab_spec = pl.BlockSpec((tm, 128), lambda i: (
        jnp.where(i < N_LAT_TILES * scale, i % (TILES_PER_BATCH * scale), TILES_PER_BATCH * scale), 0))
    n_q = MLA_H * MLA_QK_PAD
    return pl.pallas_call(
        _mla_proj_kernel,
        grid=(N_ALL_TILES * scale,),
        in_specs=[
            pl.BlockSpec((tm, D), lambda i: (i, 0)),
            pl.BlockSpec((1, D), const2),
            mod_spec(0), mod_spec(1),
            pl.BlockSpec((D, MLA_C_PAD), const2),
            pl.BlockSpec((1, MLA_QR_PAD), const2),
            pl.BlockSpec((1, MLA_KVR), const2),
            pl.BlockSpec((MLA_QR_PAD, n_q), const2),
            pl.BlockSpec((MLA_KVR, MLA_H * (MLA_NOPE + MLA_V)), const2),
            tab_spec, tab_spec, tab_spec,
        ],
        out_specs=[
            pl.BlockSpec((1, n_q, tm), lambda i: (i, 0, 0)),
            pl.BlockSpec((tm, n_q), lambda i: (i, 0)),
            pl.BlockSpec((1, MLA_H * MLA_V, tm), lambda i: (i, 0, 0)),
        ],
        out_shape=[
            jax.ShapeDtypeStruct((T_ALL // tm, n_q, tm), BF16),
            jax.ShapeDtypeStruct((T_ALL, n_q), BF16),
            jax.ShapeDtypeStruct((T_ALL // tm, MLA_H * MLA_V, tm), BF16),
        ],
        compiler_params=_cparams(("arbitrary",)),
        name=f"mla_proj_l{layer}",
    )(xs, norm_g.reshape(1, D), mod, mod, w_d, q_g, kv_g, w_uq, w_ukv, *tabs)


def _rope_tables(rot_dim):
    a = rot_dim // 2
    ch = a // 2
    pos = jnp.arange(SEQ, dtype=jnp.int32)
    rows = (pos // GRID_W).astype(F32)
    cols = (pos % GRID_W).astype(F32)
    inv = ROPE_BASE ** (-jnp.arange(0, a, 2, dtype=F32) / a)

    def axis_angles(p):
        ang = p[:, None] * inv[None, :]
        return jnp.concatenate([ang, ang], axis=-1)

    ang = jnp.concatenate([axis_angles(rows), axis_angles(cols)], axis=-1)
    cos, sin = jnp.cos(ang), jnp.sin(ang)
    lane = jnp.arange(rot_dim)
    first = (lane % a) < ch
    sin_a = jnp.where(first[None, :], -sin, 0.0)
    sin_b = jnp.where(first[None, :], 0.0, sin)
    pad = 128 - rot_dim
    cos = jnp.pad(cos, ((0, TM), (0, pad)), constant_values=1.0)
    cos = cos.at[SEQ:, :].set(1.0)
    sin_a = jnp.pad(sin_a, ((0, TM), (0, pad)))
    sin_b = jnp.pad(sin_b, ((0, TM), (0, pad)))
    return cos, sin_a, sin_b


def kernel(x, c, ctx, c_ctx, w_mod, b_mod, norm1_g, norm2_g, w_ff1, w_ff2, a_w_in, a_b_in, a_ln_g, a_w_s, a_b_s, a_w_out, b_w_qkv, b_lam_q1, b_lam_k1, b_lam_q2, b_lam_k2, b_subln_g, b_w_o, c_w_dqkv, c_q_norm_g, c_w_uq, c_kv_norm_g, c_w_ukv, c_w_o, final_g):
    xs = jnp.concatenate([x.reshape(T_LAT, D), ctx.reshape(T_CTX, D)], axis=0)
    cc = jnp.concatenate([c, c_ctx[None, :], jnp.zeros((8 - BATCH - 1, D), F32)], axis=0)
    mod = _modulation(cc, w_mod, b_mod).reshape(DEPTH, 8, 1, N_MOD * D)

    tabs_b = _rope_tables(DIFF_D)
    tabs_c = _rope_tables(MLA_ROPE)
    w_ff1_b, w_ff2_b = w_ff1.astype(BF16), w_ff2.astype(BF16)

    out = None
    for l in range(DEPTH):
        kind, idx = l % 3, l // 3
        ctx_later = any(j % 3 != 0 for j in range(l + 1, DEPTH))
        last = l == DEPTH - 1
        n_tiles = N_ALL_TILES if ctx_later else N_LAT_TILES
        if kind == 0:
            b_s = jnp.broadcast_to(a_b_s[idx][:, :, None], (GROUPS, CHUNK, CHUNK))
            t = _gmlp_mix(xs, mod, l, n_tiles, norm1_g[l], a_w_in[idx].astype(BF16), a_b_in[idx],
                          a_ln_g[idx], a_w_s[idx].astype(BF16), b_s)
            xs = _out_proj(xs, t, a_w_out[idx].astype(BF16), mod, l, n_tiles)
        elif kind == 1:
            lambda_init = 0.8 - 0.6 * math.exp(-0.3 * l)
            q_t, k, v_t = _diff_proj(xs, mod, l, norm1_g[l], b_w_qkv[idx].astype(BF16), tabs_b)
            lam_tab = jnp.concatenate([b_lam_q1[idx][None], b_lam_k1[idx][None], b_lam_q2[idx][None],
                                       b_lam_k2[idx][None], jnp.zeros((4, DIFF_D), F32)], axis=0)
            o = _diff_attn(q_t, k, v_t, lam_tab, b_subln_g[idx], lambda_init, None)
            if ctx_later:
                o = _diff_attn(q_t, k, v_t, lam_tab, b_subln_g[idx], lambda_init, o)
            xs = _out_proj(xs, o, b_w_o[idx].astype(BF16), mod, l, n_tiles)
        else:
            w_d = c_w_dqkv[idx]
            w_d = jnp.concatenate([
                jnp.pad(w_d[:, :MLA_QR], ((0, 0), (0, MLA_QR_PAD - MLA_QR))),
                w_d[:, MLA_QR:MLA_QR + MLA_KVR],
                jnp.pad(w_d[:, MLA_QR + MLA_KVR:], ((0, 0), (0, 128 - MLA_ROPE)))], axis=1).astype(BF16)
            w_uq = jnp.pad(c_w_uq[idx].reshape(MLA_QR, MLA_H, MLA_QK),
                           ((0, MLA_QR_PAD - MLA_QR), (0, 0), (0, MLA_QK_PAD - MLA_QK)))
            w_uq = w_uq.reshape(MLA_QR_PAD, MLA_H * MLA_QK_PAD).astype(BF16)
            w_ukv = c_w_ukv[idx].reshape(MLA_KVR, MLA_H, MLA_NOPE + MLA_V)
            w_ukv = jnp.concatenate([w_ukv[:, :, :MLA_NOPE].reshape(MLA_KVR, MLA_H * MLA_NOPE),
                                     w_ukv[:, :, MLA_NOPE:].reshape(MLA_KVR, MLA_H * MLA_V)],
                                    axis=1).astype(BF16)
            q_g = jnp.pad(c_q_norm_g[idx], (0, MLA_QR_PAD - MLA_QR)).reshape(1, MLA_QR_PAD)
            q_t, k, v_t = _mla_proj(xs, mod, l, norm1_g[l], w_d, q_g, c_kv_norm_g[idx].reshape(1, MLA_KVR),
                                    w_uq, w_ukv, tabs_c)
            o = _mla_attn(q_t, k, v_t)
            xs = _out_proj(xs, o, c_w_o[idx].astype(BF16), mod, l, N_LAT_TILES)
        if ctx_later and kind == 2:
            raise NotImplementedError("context update after an MLA layer is not needed at this depth")
        res = _ffn(xs, mod, l, n_tiles, norm2_g[l], w_ff1_b, w_ff2_b, final_g, last)
        if last:
            out = res
        else:
            xs = res
    return out.reshape(BATCH, SEQ, D)
```

```python
import functools
import math

import jax
import jax.numpy as jnp
from jax import lax
from jax.experimental import pallas as pl
from jax.experimental.pallas import tpu as pltpu

D = 2048
BATCH = 2
SEQ = 8192
DEPTH = 4
GRID_W = 64
CTX = 256
N_MOD = 6
D_FF = 4 * D
EPS = 1e-6
ROPE_BASE = 10000.0

CHUNK = 128
GROUPS = 16
GDIM = 128

DIFF_H = 8
DIFF_D = 128
DIFF_V = 256

MLA_H = 16
MLA_QR = 448
MLA_QR_PAD = 512
MLA_KVR = 512
MLA_NOPE = 128
MLA_ROPE = 64
MLA_V = 128
MLA_QK = MLA_NOPE + MLA_ROPE
MLA_QK_PAD = 256
MLA_C_PAD = MLA_QR_PAD + MLA_KVR + 128

T_LAT = BATCH * SEQ
T_CTX = BATCH * CTX
T_ALL = T_LAT + T_CTX

TM = 512
TILES_PER_BATCH = SEQ // TM
N_LAT_TILES = T_LAT // TM
N_ALL_TILES = T_ALL // TM
ROPE_ID_TILE = SEQ // TM

VMEM_LIMIT = 56 * 1024 * 1024

BF16 = jnp.bfloat16
F32 = jnp.float32


def _cparams(sem, flags=None):
    return pltpu.CompilerParams(dimension_semantics=sem, vmem_limit_bytes=VMEM_LIMIT, flags=flags)


ATTN_FLAGS = None


def _mod_row(i):
    return jnp.minimum(i // TILES_PER_BATCH, 2)


def _mod_spec(layer, k):
    return pl.BlockSpec((None, None, 1, D), lambda i, *_: (layer, _mod_row(i), 0, k))


def _rope_tile(i):
    return jnp.where(i < N_LAT_TILES, i % TILES_PER_BATCH, ROPE_ID_TILE)


def _norm_mod(x, g, sh, sc):
    y = x * lax.rsqrt(jnp.mean(x * x, axis=-1, keepdims=True) + EPS)
    y = y * g
    return y * (1.0 + sc) + sh


def _rope(x, cos, sin_a, sin_b, ch):
    return (x * cos + pltpu.roll(x, 128 - ch, 1) * sin_a + pltpu.roll(x, ch, 1) * sin_b)


def _mod_kernel(c_ref, w_ref, b_ref, o_ref):
    c = c_ref[...]
    s = (c * jax.nn.sigmoid(c)).astype(BF16)
    o_ref[...] = jnp.dot(s, w_ref[...].astype(BF16), preferred_element_type=F32) + b_ref[...]


def _modulation(cc, w_mod, b_mod):
    tn = 1024
    n = N_MOD * D
    return pl.pallas_call(
        _mod_kernel,
        grid=(DEPTH, n // tn),
        in_specs=[
            pl.BlockSpec((8, D), lambda l, j: (0, 0)),
            pl.BlockSpec((None, D, tn), lambda l, j: (l, 0, j)),
            pl.BlockSpec((None, 1, tn), lambda l, j: (l, 0, j)),
        ],
        out_specs=pl.BlockSpec((None, 8, tn), lambda l, j: (l, 0, j)),
        out_shape=jax.ShapeDtypeStruct((DEPTH, 8, n), F32),
        compiler_params=_cparams(("arbitrary", "arbitrary")),
        name="modulation",
    )(cc, w_mod, b_mod.reshape(DEPTH, 1, n))


def _gelu_tanh(x):
    return 0.5 * x * (1.0 + jnp.tanh(math.sqrt(2.0 / math.pi) * (x + 0.044715 * (x * x * x))))


def _gmlp_kernel(x_ref, g_ref, sh_ref, sc_ref, win_ref, bin_ref, lng_ref, ws_ref, bs_ref, o_ref,
                 vn_ref):
    tm = x_ref.shape[0]
    nchunk = tm // CHUNK
    h = _norm_mod(x_ref[...], g_ref[...], sh_ref[...], sc_ref[...]).astype(BF16)
    zv = _gelu_tanh(jnp.dot(h, win_ref[:, D:], preferred_element_type=F32) + bin_ref[:, D:])
    mu = jnp.mean(zv, axis=-1, keepdims=True)
    zc = zv - mu
    vn = zc * lax.rsqrt(jnp.mean(zc * zc, axis=-1, keepdims=True) + EPS) * lng_ref[...]
    vn_ref[...] = vn.astype(BF16)
    ublk = 512
    zus = [_gelu_tanh(jnp.dot(h, win_ref[:, ub * ublk:(ub + 1) * ublk], preferred_element_type=F32)
                      + bin_ref[:, ub * ublk:(ub + 1) * ublk]) for ub in range(D // ublk)]
    for ub, zu in enumerate(zus):
        for gg in range(ublk // GDIM):
            grp = ub * (ublk // GDIM) + gg
            cols = slice(grp * GDIM, (grp + 1) * GDIM)
            rhs = jnp.concatenate([vn_ref[c * CHUNK:(c + 1) * CHUNK, cols] for c in range(nchunk)], axis=1)
            sv = jnp.dot(ws_ref[grp], rhs, preferred_element_type=F32)
            for c in range(nchunk):
                svc = sv[:, c * CHUNK:(c + 1) * CHUNK] + bs_ref[grp]
                u = zu[c * CHUNK:(c + 1) * CHUNK, gg * GDIM:(gg + 1) * GDIM]
                o_ref[c * CHUNK:(c + 1) * CHUNK, cols] = (u * svc).astype(BF16)


def _gmlp_mix(xs, mod, layer, n_tiles, norm_g, w_in, b_in, ln_g, w_s, b_s):
    tm = 512
    scale = TM // tm
    const2 = lambda i: (0, 0)
    mod_spec = lambda k: pl.BlockSpec((None, None, 1, D), lambda i: (layer, _mod_row(i // scale), 0, k))
    return pl.pallas_call(
        _gmlp_kernel,
        grid=(n_tiles * scale,),
        in_specs=[
            pl.BlockSpec((tm, D), lambda i: (i, 0)),
            pl.BlockSpec((1, D), const2),
            mod_spec(0), mod_spec(1),
            pl.BlockSpec((D, 2 * D), const2),
            pl.BlockSpec((1, 2 * D), const2),
            pl.BlockSpec((1, D), const2),
            pl.BlockSpec((GROUPS, CHUNK, CHUNK), lambda i: (0, 0, 0)),
            pl.BlockSpec((GROUPS, CHUNK, CHUNK), lambda i: (0, 0, 0)),
        ],
        out_specs=pl.BlockSpec((tm, D), lambda i: (i, 0)),
        out_shape=jax.ShapeDtypeStruct((T_ALL, D), BF16),
        scratch_shapes=[pltpu.VMEM((tm, D), BF16)],
        compiler_params=_cparams(("arbitrary",)),
        name=f"gmlp_mix_l{layer}",
    )(xs, norm_g.reshape(1, D), mod, mod, w_in, b_in.reshape(1, 2 * D), ln_g.reshape(1, D), w_s, b_s)


def _out_proj_kernel(x_ref, a_ref, w_ref, gate_ref, o_ref):
    y = jnp.dot(a_ref[...], w_ref[...], preferred_element_type=F32)
    o_ref[...] = x_ref[...] + gate_ref[...] * y


def _out_proj(xs, a, w, mod, layer, n_tiles):
    ka = a.shape[1]
    return pl.pallas_call(
        _out_proj_kernel,
        grid=(n_tiles,),
        in_specs=[
            pl.BlockSpec((TM, D), lambda i: (i, 0)),
            pl.BlockSpec((TM, ka), lambda i: (i, 0)),
            pl.BlockSpec((ka, D), lambda i: (0, 0)),
            _mod_spec(layer, 2),
        ],
        out_specs=pl.BlockSpec((TM, D), lambda i: (i, 0)),
        out_shape=jax.ShapeDtypeStruct((T_ALL, D), F32),
        input_output_aliases={0: 0},
        compiler_params=_cparams(("arbitrary",)),
        name=f"out_proj_l{layer}",
    )(xs, a, w, mod)


def _ffn_kernel(x_ref, xn_ref, g_ref, sh_ref, sc_ref, shn_ref, scn_ref, gate_ref, w1_ref, w2_ref, fg_ref,
                o_ref, h_ref, hn_ref, *, n_f, final_norm):
    i = pl.program_id(0)
    f = pl.program_id(1)

    @pl.when(jnp.logical_and(i == 0, f == 0))
    def _():
        h_ref[...] = _norm_mod(x_ref[...], g_ref[...], sh_ref[...], sc_ref[...]).astype(BF16)

    @pl.when(jnp.logical_and(i > 0, f == 0))
    def _():
        h_ref[...] = hn_ref[...]

    @pl.when(f == 0)
    def _():
        o_ref[...] = jnp.zeros_like(o_ref)

    band = TM // n_f
    rows = pl.ds(pl.multiple_of(f * band, band), band)
    hn_ref[rows, :] = _norm_mod(xn_ref[rows, :], g_ref[...], shn_ref[...], scn_ref[...]).astype(BF16)

    a = jnp.dot(h_ref[...], w1_ref[...], preferred_element_type=F32)
    a = jnp.square(jnp.maximum(a, 0.0)).astype(BF16)
    o_ref[...] += jnp.dot(a, w2_ref[...], preferred_element_type=F32)

    @pl.when(f == n_f - 1)
    def _():
        y = x_ref[...] + gate_ref[...] * o_ref[...]
        if final_norm:
            y = y * lax.rsqrt(jnp.mean(y * y, axis=-1, keepdims=True) + EPS) * fg_ref[...]
        o_ref[...] = y


def _ffn(xs, mod, layer, n_tiles, norm_g, w1, w2, final_g, final_norm):
    tf = 1024
    n_f = D_FF // tf
    rows = T_LAT if final_norm else T_ALL
    const2 = lambda i, f: (0, 0)
    nxt = lambda i: jnp.minimum(i + 1, n_tiles - 1)
    mod_next = lambda k: pl.BlockSpec((None, None, 1, D), lambda i, f: (layer, _mod_row(nxt(i)), 0, k))
    return pl.pallas_call(
        functools.partial(_ffn_kernel, n_f=n_f, final_norm=final_norm),
        grid=(n_tiles, n_f),
        in_specs=[
            pl.BlockSpec((TM, D), lambda i, f: (i, 0)),
            pl.BlockSpec((TM, D), lambda i, f: (nxt(i), 0)),
            pl.BlockSpec((1, D), const2),
            _mod_spec(layer, 3), _mod_spec(layer, 4), mod_next(3), mod_next(4), _mod_spec(layer, 5),
            pl.BlockSpec((None, D, tf), lambda i, f: (layer, 0, f)),
            pl.BlockSpec((None, tf, D), lambda i, f: (layer, f, 0)),
            pl.BlockSpec((1, D), const2),
        ],
        out_specs=pl.BlockSpec((TM, D), lambda i, f: (i, 0)),
        out_shape=jax.ShapeDtypeStruct((rows, D), F32),
        scratch_shapes=[pltpu.VMEM((TM, D), BF16), pltpu.VMEM((TM, D), BF16)],
        compiler_params=_cparams(("arbitrary", "arbitrary")),
        name=f"ffn_l{layer}",
    )(xs, xs, norm_g.reshape(1, D), mod, mod, mod, mod, mod, w1, w2, final_g.reshape(1, D))


TB = 256
LOG2E = 1.4426950408889634
DIFF_Q_SCALE = DIFF_D ** -0.5 * LOG2E
MLA_Q_SCALE = MLA_QK ** -0.5 * LOG2E
DIFF_BAND = DIFF_H * DIFF_V
PROJ_SUB = 512


def _diff_proj_kernel(x_ref, g_ref, sh_ref, sc_ref, w_ref, cos_ref, sa_ref, sb_ref,
                      qt_ref, k_ref, vt_ref, h_ref):
    n = pl.program_id(1)

    @pl.when(n == 0)
    def _():
        h_ref[...] = _norm_mod(x_ref[...], g_ref[...], sh_ref[...], sc_ref[...]).astype(BF16)

    def sub_dot(s):
        return jnp.dot(h_ref[...], w_ref[:, s * PROJ_SUB:(s + 1) * PROJ_SUB], preferred_element_type=F32)

    def roped(y):
        cos, sa, sb = cos_ref[...], sa_ref[...], sb_ref[...]
        return jnp.concatenate([_rope(y[:, c * 128:(c + 1) * 128], cos, sa, sb, 32)
                                for c in range(PROJ_SUB // 128)], axis=1)

    def store_t(ref, s, y):
        for t in range(TM // TB):
            ref[t, s * PROJ_SUB:(s + 1) * PROJ_SUB, :] = y[t * TB:(t + 1) * TB, :].T.astype(BF16)

    @pl.when(n == 0)
    def _():
        for s in range(DIFF_BAND // PROJ_SUB):
            store_t(qt_ref, s, roped(sub_dot(s) * DIFF_Q_SCALE))

    @pl.when(n == 1)
    def _():
        for s in range(DIFF_BAND // PROJ_SUB):
            k_ref[:, s * PROJ_SUB:(s + 1) * PROJ_SUB] = roped(sub_dot(s)).astype(BF16)

    @pl.when(n == 2)
    def _():
        for s in range(DIFF_BAND // PROJ_SUB):
            store_t(vt_ref, s, sub_dot(s))


def _diff_proj(xs, mod, layer, norm_g, w_qkv, tabs):
    const2 = lambda i, n: (0, 0)
    tab_spec = pl.BlockSpec((TM, 128), lambda i, n: (_rope_tile(i), 0))
    t_spec = pl.BlockSpec((TM // TB, DIFF_BAND, TB), lambda i, n: (i, 0, 0))
    t_shape = jax.ShapeDtypeStruct((T_ALL // TB, DIFF_BAND, TB), BF16)
    return pl.pallas_call(
        _diff_proj_kernel,
        grid=(N_ALL_TILES, 3),
        in_specs=[
            pl.BlockSpec((TM, D), lambda i, n: (i, 0)),
            pl.BlockSpec((1, D), const2),
            _mod_spec(layer, 0), _mod_spec(layer, 1),
            pl.BlockSpec((D, DIFF_BAND), lambda i, n: (0, n)),
            tab_spec, tab_spec, tab_spec,
        ],
        out_specs=[t_spec, pl.BlockSpec((TM, DIFF_BAND), lambda i, n: (i, 0)), t_spec],
        out_shape=[t_shape, jax.ShapeDtypeStruct((T_ALL, DIFF_BAND), BF16), t_shape],
        scratch_shapes=[pltpu.VMEM((TM, D), BF16)],
        compiler_params=_cparams(("arbitrary", "arbitrary")),
        name=f"diff_proj_l{layer}",
    )(xs, norm_g.reshape(1, D), mod, mod, w_qkv, *tabs)


TQ = 1024
TK = 1024
SUM_ROWS = 16


def _scores(k, q_t):
    return jnp.dot(k, q_t, preferred_element_type=F32)


def _attn_update(s, m_blk, v_t_parts, state, first):
    m_ref, l_ref, acc_ref = state
    dv = acc_ref.shape[0]
    m_new = m_blk if first else jnp.maximum(m_ref[...], m_blk)
    pb = jnp.exp2(s - m_new).astype(BF16)
    ones = jnp.ones((SUM_ROWS, TB), BF16)
    pv = None
    for t, v_t in enumerate(v_t_parts):
        d = jnp.dot(jnp.concatenate([v_t, ones], axis=0), pb[t * TB:(t + 1) * TB, :],
                    preferred_element_type=F32)
        pv = d if pv is None else pv + d
    l_blk = pv[dv:dv + 1, :]
    if first:
        l_ref[...] = l_blk
        acc_ref[...] = pv[:dv, :]
    else:
        alpha = jnp.exp2(m_ref[...] - m_new)
        l_ref[...] = alpha * l_ref[...] + l_blk
        acc_ref[...] = alpha * acc_ref[...] + pv[:dv, :]
    m_ref[...] = m_new


def _attn_pipeline(score_fn, v_fn, head_scores, head_v, states, slot_a, slot_b, n):
    def stash(j, slot):
        for s, (s_ref, mb_ref) in zip(score_fn(j), slot):
            s_ref[...] = s
            mb_ref[...] = jnp.max(s, axis=0, keepdims=True)

    def update(j, slot):
        for (s_ref, mb_ref), v_parts, state in zip(slot, v_fn(j), states):
            _attn_update(s_ref[...], mb_ref[...], v_parts, state, False)

    stash(0, slot_a)
    for s, v_parts, state in zip(head_scores, head_v, states):
        _attn_update(s, jnp.max(s, axis=0, keepdims=True), v_parts, state, True)

    def pair(i, carry):
        j = 2 * i
        stash(j + 1, slot_b)
        update(j, slot_a)
        stash(j + 2, slot_a)
        update(j + 1, slot_b)
        return carry

    lax.fori_loop(0, n // 2 - 1, pair, 0)
    stash(n - 1, slot_b)
    update(n - 2, slot_a)
    update(n - 1, slot_b)


def _load_q_t(qt_ref, rows):
    return jnp.concatenate([qt_ref[t, rows, :] for t in range(qt_ref.shape[0])], axis=1)


def _lat_chunk_fns(k_ref, vt_ref, v_rows):
    k_fn = lambda j: k_ref[pl.ds(pl.multiple_of(j * TK, TK), TK), :]
    v_fn = lambda j: [[vt_ref[(TK // TB) * j + t, rows, :] for t in range(TK // TB)] for rows in v_rows]
    return k_fn, v_fn


def _slot_scratch(n_chain, tq):
    return [pltpu.VMEM((TK, tq), F32), pltpu.VMEM((1, tq), F32)] * (2 * n_chain)


def _split_slots(refs, n_chain):
    pairs = [(refs[2 * i], refs[2 * i + 1]) for i in range(2 * n_chain)]
    return pairs[:n_chain], pairs[n_chain:]


def _diff_attn_kernel(*refs, has_lat, lambda_init):
    if has_lat:
        qt_ref, kc_ref, vtc_ref, k_ref, vt_ref, lam_ref, sg_ref, o_ref = refs[:8]
        scratch = refs[8:]
    else:
        qt_ref, kc_ref, vtc_ref, lam_ref, sg_ref, o_ref = refs[:6]
        scratch = refs[6:]
    states = (scratch[0:3], scratch[3:6])
    halves = (slice(0, DIFF_D), slice(DIFF_D, 2 * DIFF_D))

    def scores_of(k):
        return [_scores(k[:, halves[idx]], _load_q_t(qt_ref, halves[idx])) for idx in range(2)]

    ctx_scores = scores_of(kc_ref[...])
    v_rows = [slice(0, DIFF_V)] * 2
    ctx_v = [[vtc_ref[0]]] * 2
    if has_lat:
        k_fn, v_fn = _lat_chunk_fns(k_ref, vt_ref, v_rows)
        slot_a, slot_b = _split_slots(scratch[6:], 2)
        _attn_pipeline(lambda j: scores_of(k_fn(j)), v_fn, ctx_scores, ctx_v, states,
                       slot_a, slot_b, SEQ // TK)
    else:
        for s, v_parts, state in zip(ctx_scores, ctx_v, states):
            _attn_update(s, jnp.max(s, axis=0, keepdims=True), v_parts, state, True)

    lam_v = lam_ref[...]
    lam = (jnp.exp(jnp.sum(lam_v[0:1] * lam_v[1:2], axis=-1, keepdims=True))
           - jnp.exp(jnp.sum(lam_v[2:3] * lam_v[3:4], axis=-1, keepdims=True)) + lambda_init)
    (_, l0, a0), (_, l1, a1) = states
    o_t = a0[...] * (1.0 / l0[...]) - lam * (a1[...] * (1.0 / l1[...]))
    o = o_t.T
    o = o * lax.rsqrt(jnp.mean(o * o, axis=-1, keepdims=True) + EPS) * sg_ref[...]
    o_ref[...] = (o * (1.0 - lambda_init)).astype(BF16)


def _diff_attn(q_t, k, v_t, lam_tab, subln_g, lambda_init, o_prev):
    has_lat = o_prev is None
    ctx_blk0 = T_LAT // CTX
    if has_lat:
        tq = TQ
        grid = (BATCH, DIFF_H, SEQ // tq)
        q_blk = lambda b, h, i: b * (SEQ // tq) + i
    else:
        tq = CTX
        grid = (BATCH, DIFF_H, 1)
        q_blk = lambda b, h, i: ctx_blk0 + b
    in_specs = [
        pl.BlockSpec((tq // TB, DIFF_V, TB), lambda b, h, i: (q_blk(b, h, i), h, 0)),
        pl.BlockSpec((CTX, DIFF_V), lambda b, h, i: (ctx_blk0 + b, h)),
        pl.BlockSpec((1, DIFF_V, TB), lambda b, h, i: (ctx_blk0 + b, h, 0)),
    ]
    args = [q_t, k, v_t]
    if has_lat:
        in_specs += [
            pl.BlockSpec((SEQ, DIFF_V), lambda b, h, i: (b, h)),
            pl.BlockSpec((SEQ // TB, DIFF_V, TB), lambda b, h, i: (b, h, 0)),
        ]
        args += [k, v_t]
    in_specs += [
        pl.BlockSpec((8, DIFF_D), lambda b, h, i: (0, 0)),
        pl.BlockSpec((1, DIFF_V), lambda b, h, i: (0, 0)),
    ]
    args += [lam_tab, subln_g.reshape(1, DIFF_V)]
    aliases = {}
    kern = functools.partial(_diff_attn_kernel, has_lat=has_lat, lambda_init=lambda_init)
    if not has_lat:
        in_specs.append(pl.BlockSpec(memory_space=pl.ANY))
        args.append(o_prev)
        aliases = {len(args) - 1: 0}
        base = kern
        kern = lambda *refs: base(*refs[:5], *refs[6:])
    return pl.pallas_call(
        kern,
        grid=grid,
        in_specs=in_specs,
        out_specs=pl.BlockSpec((tq, DIFF_V), lambda b, h, i: (q_blk(b, h, i), h)),
        out_shape=jax.ShapeDtypeStruct((T_ALL, DIFF_H * DIFF_V), BF16),
        input_output_aliases=aliases,
        scratch_shapes=([pltpu.VMEM((1, tq), F32), pltpu.VMEM((1, tq), F32), pltpu.VMEM((DIFF_V, tq), F32)] * 2
                        + (_slot_scratch(2, tq) if has_lat else [])),
        compiler_params=_cparams(("arbitrary",) * 3, ATTN_FLAGS),
        name="diff_attn_lat" if has_lat else "diff_attn_ctx",
    )(*args)


MLA_HPS = 2


def _mla_attn_kernel(qt_ref, kc_ref, vtc_ref, k_ref, vt_ref, o_ref, *scratch):
    states = [scratch[3 * i:3 * i + 3] for i in range(MLA_HPS)]
    qk = [slice(i * MLA_QK_PAD, (i + 1) * MLA_QK_PAD) for i in range(MLA_HPS)]
    v_rows = [slice(i * MLA_V, (i + 1) * MLA_V) for i in range(MLA_HPS)]

    def scores_of(k):
        return [_scores(k[:, qk[i]], _load_q_t(qt_ref, qk[i])) for i in range(MLA_HPS)]

    k_fn, v_fn = _lat_chunk_fns(k_ref, vt_ref, v_rows)
    slot_a, slot_b = _split_slots(scratch[3 * MLA_HPS:], MLA_HPS)
    _attn_pipeline(lambda j: scores_of(k_fn(j)), v_fn, scores_of(kc_ref[...]),
                   [[vtc_ref[0, rows, :]] for rows in v_rows], states, slot_a, slot_b, SEQ // TK)
    for i, (_, l_ref, acc_ref) in enumerate(states):
        o_ref[:, v_rows[i]] = (acc_ref[...] * (1.0 / l_ref[...])).T.astype(BF16)


def _mla_attn(q_t, k, v_t):
    ctx_blk0 = T_LAT // CTX
    qk_w, v_w = MLA_HPS * MLA_QK_PAD, MLA_HPS * MLA_V
    return pl.pallas_call(
        _mla_attn_kernel,
        grid=(BATCH, MLA_H // MLA_HPS, SEQ // TQ),
        in_specs=[
            pl.BlockSpec((TQ // TB, qk_w, TB), lambda b, h, i: (b * (SEQ // TQ) + i, h, 0)),
            pl.BlockSpec((CTX, qk_w), lambda b, h, i: (ctx_blk0 + b, h)),
            pl.BlockSpec((1, v_w, TB), lambda b, h, i: (ctx_blk0 + b, h, 0)),
            pl.BlockSpec((SEQ, qk_w), lambda b, h, i: (b, h)),
            pl.BlockSpec((SEQ // TB, v_w, TB), lambda b, h, i: (b, h, 0)),
        ],
        out_specs=pl.BlockSpec((TQ, v_w), lambda b, h, i: (b * (SEQ // TQ) + i, h)),
        out_shape=jax.ShapeDtypeStruct((T_LAT, MLA_H * MLA_V), BF16),
        scratch_shapes=([pltpu.VMEM((1, TQ), F32), pltpu.VMEM((1, TQ), F32), pltpu.VMEM((MLA_V, TQ), F32)]
                        * MLA_HPS + _slot_scratch(MLA_HPS, TQ)),
        compiler_params=_cparams(("arbitrary",) * 3, ATTN_FLAGS),
        name="mla_attn",
    )(q_t, k, v_t, k, v_t)


def _mla_proj_kernel(x_ref, g_ref, sh_ref, sc_ref, wd_ref, qg_ref, kvg_ref, wuq_ref, wukv_ref,
                     cos_ref, sa_ref, sb_ref, qt_ref, k_ref, vt_ref):
    h = _norm_mod(x_ref[...], g_ref[...], sh_ref[...], sc_ref[...]).astype(BF16)
    c = jnp.dot(h, wd_ref[...], preferred_element_type=F32)
    cq = c[:, :MLA_QR_PAD]
    ckv = c[:, MLA_QR_PAD:MLA_QR_PAD + MLA_KVR]
    kr = c[:, MLA_QR_PAD + MLA_KVR:]
    cqn = cq * lax.rsqrt(jnp.sum(cq * cq, axis=-1, keepdims=True) * (1.0 / MLA_QR) + EPS) * qg_ref[...]
    ckvn = ckv * lax.rsqrt(jnp.mean(ckv * ckv, axis=-1, keepdims=True) + EPS) * kvg_ref[...]
    cos, sa, sb = cos_ref[...], sa_ref[...], sb_ref[...]
    kr_rot = _rope(kr, cos, sa, sb, 16).astype(BF16)
    q = jnp.dot(cqn.astype(BF16), wuq_ref[...], preferred_element_type=F32) * MLA_Q_SCALE
    kv = jnp.dot(ckvn.astype(BF16), wukv_ref[...], preferred_element_type=F32)
    for hd in range(MLA_H):
        base = hd * MLA_QK_PAD
        qt_ref[0, base:base + 128, :] = q[:, base:base + 128].T.astype(BF16)
        qt_ref[0, base + 128:base + 256, :] = _rope(q[:, base + 128:base + 256], cos, sa, sb, 16).T.astype(BF16)
        k_ref[:, base:base + 128] = kv[:, hd * MLA_NOPE:(hd + 1) * MLA_NOPE].astype(BF16)
        k_ref[:, base + 128:base + 256] = kr_rot
        vcols = slice(MLA_H * MLA_NOPE + hd * MLA_V, MLA_H * MLA_NOPE + (hd + 1) * MLA_V)
        vt_ref[0, hd * MLA_V:(hd + 1) * MLA_V, :] = kv[:, vcols].T.astype(BF16)


def _mla_proj(xs, mod, layer, norm_g, w_d, q_g, kv_g, w_uq, w_ukv, tabs):
    tm = TB
    scale = TM // tm
    const2 = lambda i: (0, 0)
    mod_spec = lambda k: pl.BlockSpec((None, None, 1, D), lambda i: (layer, _mod_row(i // scale), 0, k))
    tab_spec = pl.BlockSpec((tm, 128), lambda i: (
        jnp.where(i < N_LAT_TILES * scale, i % (TILES_PER_BATCH * scale), TILES_PER_BATCH * scale), 0))
    n_q = MLA_H * MLA_QK_PAD
    return pl.pallas_call(
        _mla_proj_kernel,
        grid=(N_ALL_TILES * scale,),
        in_specs=[
            pl.BlockSpec((tm, D), lambda i: (i, 0)),
            pl.BlockSpec((1, D), const2),
            mod_spec(0), mod_spec(1),
            pl.BlockSpec((D, MLA_C_PAD), const2),
            pl.BlockSpec((1, MLA_QR_PAD), const2),
            pl.BlockSpec((1, MLA_KVR), const2),
            pl.BlockSpec((MLA_QR_PAD, n_q), const2),
            pl.BlockSpec((MLA_KVR, MLA_H * (MLA_NOPE + MLA_V)), const2),
            tab_spec, tab_spec, tab_spec,
        ],
        out_specs=[
            pl.BlockSpec((1, n_q, tm), lambda i: (i, 0, 0)),
            pl.BlockSpec((tm, n_q), lambda i: (i, 0)),
            pl.BlockSpec((1, MLA_H * MLA_V, tm), lambda i: (i, 0, 0)),
        ],
        out_shape=[
            jax.ShapeDtypeStruct((T_ALL // tm, n_q, tm), BF16),
            jax.ShapeDtypeStruct((T_ALL, n_q), BF16),
            jax.ShapeDtypeStruct((T_ALL // tm, MLA_H * MLA_V, tm), BF16),
        ],
        compiler_params=_cparams(("arbitrary",)),
        name=f"mla_proj_l{layer}",
    )(xs, norm_g.reshape(1, D), mod, mod, w_d, q_g, kv_g, w_uq, w_ukv, *tabs)


def _rope_tables(rot_dim):
    a = rot_dim // 2
    ch = a // 2
    pos = jnp.arange(SEQ, dtype=jnp.int32)
    rows = (pos // GRID_W).astype(F32)
    cols = (pos % GRID_W).astype(F32)
    inv = ROPE_BASE ** (-jnp.arange(0, a, 2, dtype=F32) / a)

    def axis_angles(p):
        ang = p[:, None] * inv[None, :]
        return jnp.concatenate([ang, ang], axis=-1)

    ang = jnp.concatenate([axis_angles(rows), axis_angles(cols)], axis=-1)
    cos, sin = jnp.cos(ang), jnp.sin(ang)
    lane = jnp.arange(rot_dim)
    first = (lane % a) < ch
    sin_a = jnp.where(first[None, :], -sin, 0.0)
    sin_b = jnp.where(first[None, :], 0.0, sin)
    pad = 128 - rot_dim
    cos = jnp.pad(cos, ((0, TM), (0, pad)), constant_values=1.0)
    cos = cos.at[SEQ:, :].set(1.0)
    sin_a = jnp.pad(sin_a, ((0, TM), (0, pad)))
    sin_b = jnp.pad(sin_b, ((0, TM), (0, pad)))
    return cos, sin_a, sin_b


def kernel(x, c, ctx, c_ctx, w_mod, b_mod, norm1_g, norm2_g, w_ff1, w_ff2, a_w_in, a_b_in, a_ln_g, a_w_s, a_b_s, a_w_out, b_w_qkv, b_lam_q1, b_lam_k1, b_lam_q2, b_lam_k2, b_subln_g, b_w_o, c_w_dqkv, c_q_norm_g, c_w_uq, c_kv_norm_g, c_w_ukv, c_w_o, final_g):
    xs = jnp.concatenate([x.reshape(T_LAT, D), ctx.reshape(T_CTX, D)], axis=0)
    cc = jnp.concatenate([c, c_ctx[None, :], jnp.zeros((8 - BATCH - 1, D), F32)], axis=0)
    mod = _modulation(cc, w_mod, b_mod).reshape(DEPTH, 8, 1, N_MOD * D)

    tabs_b = _rope_tables(DIFF_D)
    tabs_c = _rope_tables(MLA_ROPE)
    w_ff1_b, w_ff2_b = w_ff1.astype(BF16), w_ff2.astype(BF16)

    out = None
    for l in range(DEPTH):
        kind, idx = l % 3, l // 3
        ctx_later = any(j % 3 != 0 for j in range(l + 1, DEPTH))
        last = l == DEPTH - 1
        n_tiles = N_ALL_TILES if ctx_later else N_LAT_TILES
        if kind == 0:
            b_s = jnp.broadcast_to(a_b_s[idx][:, :, None], (GROUPS, CHUNK, CHUNK))
            t = _gmlp_mix(xs, mod, l, n_tiles, norm1_g[l], a_w_in[idx].astype(BF16), a_b_in[idx],
                          a_ln_g[idx], a_w_s[idx].astype(BF16), b_s)
            xs = _out_proj(xs, t, a_w_out[idx].astype(BF16), mod, l, n_tiles)
        elif kind == 1:
            lambda_init = 0.8 - 0.6 * math.exp(-0.3 * l)
            q_t, k, v_t = _diff_proj(xs, mod, l, norm1_g[l], b_w_qkv[idx].astype(BF16), tabs_b)
            lam_tab = jnp.concatenate([b_lam_q1[idx][None], b_lam_k1[idx][None], b_lam_q2[idx][None],
                                       b_lam_k2[idx][None], jnp.zeros((4, DIFF_D), F32)], axis=0)
            o = _diff_attn(q_t, k, v_t, lam_tab, b_subln_g[idx], lambda_init, None)
            if ctx_later:
                o = _diff_attn(q_t, k, v_t, lam_tab, b_subln_g[idx], lambda_init, o)
            xs = _out_proj(xs, o, b_w_o[idx].astype(BF16), mod, l, n_tiles)
        else:
            w_d = c_w_dqkv[idx]
            w_d = jnp.concatenate([
                jnp.pad(w_d[:, :MLA_QR], ((0, 0), (0, MLA_QR_PAD - MLA_QR))),
                w_d[:, MLA_QR:MLA_QR + MLA_KVR],
                jnp.pad(w_d[:, MLA_QR + MLA_KVR:], ((0, 0), (0, 128 - MLA_ROPE)))], axis=1).astype(BF16)
            w_uq = jnp.pad(c_w_uq[idx].reshape(MLA_QR, MLA_H, MLA_QK),
                           ((0, MLA_QR_PAD - MLA_QR), (0, 0), (0, MLA_QK_PAD - MLA_QK)))
            w_uq = w_uq.reshape(MLA_QR_PAD, MLA_H * MLA_QK_PAD).astype(BF16)
            w_ukv = c_w_ukv[idx].reshape(MLA_KVR, MLA_H, MLA_NOPE + MLA_V)
            w_ukv = jnp.concatenate([w_ukv[:, :, :MLA_NOPE].reshape(MLA_KVR, MLA_H * MLA_NOPE),
                                     w_ukv[:, :, MLA_NOPE:].reshape(MLA_KVR, MLA_H * MLA_V)],
                                    axis=1).astype(BF16)
            q_g = jnp.pad(c_q_norm_g[idx], (0, MLA_QR_PAD - MLA_QR)).reshape(1, MLA_QR_PAD)
            q_t, k, v_t = _mla_proj(xs, mod, l, norm1_g[l], w_d, q_g, c_kv_norm_g[idx].reshape(1, MLA_KVR),
                                    w_uq, w_ukv, tabs_c)
            o = _mla_attn(q_t, k, v_t)
            xs = _out_proj(xs, o, c_w_o[idx].astype(BF16), mod, l, N_LAT_TILES)
        if ctx_later and kind == 2:
            raise NotImplementedError("context update after an MLA layer is not needed at this depth")
        res = _ffn(xs, mod, l, n_tiles, norm2_g[l], w_ff1_b, w_ff2_b, final_g, last)
        if last:
            out = res
        else:
            xs = res
    return out.reshape(BATCH, SEQ, D)
```

```python
import functools
import math

import jax
import jax.numpy as jnp
from jax import lax
from jax.experimental import pallas as pl
from jax.experimental.pallas import tpu as pltpu

D = 2048
BATCH = 2
SEQ = 8192
DEPTH = 4
GRID_W = 64
CTX = 256
N_MOD = 6
D_FF = 4 * D
EPS = 1e-6
ROPE_BASE = 10000.0

CHUNK = 128
GROUPS = 16
GDIM = 128

DIFF_H = 8
DIFF_D = 128
DIFF_V = 256

MLA_H = 16
MLA_QR = 448
MLA_QR_PAD = 512
MLA_KVR = 512
MLA_NOPE = 128
MLA_ROPE = 64
MLA_V = 128
MLA_QK = MLA_NOPE + MLA_ROPE
MLA_QK_PAD = 256
MLA_C_PAD = MLA_QR_PAD + MLA_KVR + 128

T_LAT = BATCH * SEQ
T_CTX = BATCH * CTX
T_ALL = T_LAT + T_CTX

TM = 512
TILES_PER_BATCH = SEQ // TM
N_LAT_TILES = T_LAT // TM
N_ALL_TILES = T_ALL // TM
ROPE_ID_TILE = SEQ // TM

VMEM_LIMIT = 56 * 1024 * 1024

BF16 = jnp.bfloat16
F32 = jnp.float32


def _cparams(sem, flags=None):
    return pltpu.CompilerParams(dimension_semantics=sem, vmem_limit_bytes=VMEM_LIMIT, flags=flags)


ATTN_FLAGS = None


def _mod_row(i):
    return jnp.minimum(i // TILES_PER_BATCH, 2)


def _mod_spec(layer, k):
    return pl.BlockSpec((None, None, 1, D), lambda i, *_: (layer, _mod_row(i), 0, k))


def _rope_tile(i):
    return jnp.where(i < N_LAT_TILES, i % TILES_PER_BATCH, ROPE_ID_TILE)


def _norm_mod(x, g, sh, sc):
    y = x * lax.rsqrt(jnp.mean(x * x, axis=-1, keepdims=True) + EPS)
    y = y * g
    return y * (1.0 + sc) + sh


def _token_specs(xs):
    if isinstance(xs, tuple):
        return ([pl.BlockSpec((TM, D), lambda i, *_: (jnp.minimum(i, N_LAT_TILES - 1), 0)),
                 pl.BlockSpec((TM, D), lambda i, *_: (jnp.maximum(i - N_LAT_TILES, 0), 0))], list(xs))
    return [pl.BlockSpec((TM, D), lambda i, *_: (i, 0))], [xs]


def _token_tile(src_refs):
    if len(src_refs) == 2:
        return jnp.where(pl.program_id(0) < N_LAT_TILES, src_refs[0][...], src_refs[1][...])
    return src_refs[0][...]


def _rope(x, cos, sin_a, sin_b, ch):
    return (x * cos + pltpu.roll(x, 128 - ch, 1) * sin_a + pltpu.roll(x, ch, 1) * sin_b)


def _mod_kernel(c_ref, w_ref, b_ref, o_ref):
    c = c_ref[...]
    s = (c * jax.nn.sigmoid(c)).astype(BF16)
    o_ref[...] = jnp.dot(s, w_ref[...].astype(BF16), preferred_element_type=F32) + b_ref[...]


def _modulation(cc, w_mod, b_mod):
    tn = 1024
    n = N_MOD * D
    return pl.pallas_call(
        _mod_kernel,
        grid=(DEPTH, n // tn),
        in_specs=[
            pl.BlockSpec((8, D), lambda l, j: (0, 0)),
            pl.BlockSpec((None, D, tn), lambda l, j: (l, 0, j)),
            pl.BlockSpec((None, 1, tn), lambda l, j: (l, 0, j)),
        ],
        out_specs=pl.BlockSpec((None, 8, tn), lambda l, j: (l, 0, j)),
        out_shape=jax.ShapeDtypeStruct((DEPTH, 8, n), F32),
        compiler_params=_cparams(("arbitrary", "arbitrary")),
        name="modulation",
    )(cc, w_mod, b_mod.reshape(DEPTH, 1, n))


def _gelu_tanh(x):
    return 0.5 * x * (1.0 + jnp.tanh(math.sqrt(2.0 / math.pi) * (x + 0.044715 * (x * x * x))))


def _gmlp_kernel(*refs, n_src):
    src = refs[:n_src]
    g_ref, sh_ref, sc_ref, win_ref, bin_ref, lng_ref, ws_ref, bs_ref, o_ref, vn_ref = refs[n_src:]
    nchunk = TM // CHUNK
    h = _norm_mod(_token_tile(src), g_ref[...], sh_ref[...], sc_ref[...]).astype(BF16)
    zv = _gelu_tanh(jnp.dot(h, win_ref[:, D:], preferred_element_type=F32) + bin_ref[:, D:])
    mu = jnp.mean(zv, axis=-1, keepdims=True)
    zc = zv - mu
    vn = zc * lax.rsqrt(jnp.mean(zc * zc, axis=-1, keepdims=True) + EPS) * lng_ref[...]
    vn_ref[...] = vn.astype(BF16)
    ublk = 512
    zus = [_gelu_tanh(jnp.dot(h, win_ref[:, ub * ublk:(ub + 1) * ublk], preferred_element_type=F32)
                      + bin_ref[:, ub * ublk:(ub + 1) * ublk]) for ub in range(D // ublk)]
    for ub, zu in enumerate(zus):
        for gg in range(ublk // GDIM):
            grp = ub * (ublk // GDIM) + gg
            cols = slice(grp * GDIM, (grp + 1) * GDIM)
            rhs = jnp.concatenate([vn_ref[c * CHUNK:(c + 1) * CHUNK, cols] for c in range(nchunk)], axis=1)
            sv = jnp.dot(ws_ref[grp], rhs, preferred_element_type=F32)
            for c in range(nchunk):
                svc = sv[:, c * CHUNK:(c + 1) * CHUNK] + bs_ref[grp]
                u = zu[c * CHUNK:(c + 1) * CHUNK, gg * GDIM:(gg + 1) * GDIM]
                o_ref[c * CHUNK:(c + 1) * CHUNK, cols] = (u * svc).astype(BF16)


def _gmlp_mix(xs, mod, layer, n_tiles, norm_g, w_in, b_in, ln_g, w_s, b_s):
    const2 = lambda i: (0, 0)
    src_specs, src_args = _token_specs(xs)
    return pl.pallas_call(
        functools.partial(_gmlp_kernel, n_src=len(src_args)),
        grid=(n_tiles,),
        in_specs=src_specs + [
            pl.BlockSpec((1, D), const2),
            _mod_spec(layer, 0), _mod_spec(layer, 1),
            pl.BlockSpec((D, 2 * D), const2),
            pl.BlockSpec((1, 2 * D), const2),
            pl.BlockSpec((1, D), const2),
            pl.BlockSpec((GROUPS, CHUNK, CHUNK), lambda i: (0, 0, 0)),
            pl.BlockSpec((GROUPS, CHUNK, CHUNK), lambda i: (0, 0, 0)),
        ],
        out_specs=pl.BlockSpec((TM, D), lambda i: (i, 0)),
        out_shape=jax.ShapeDtypeStruct((T_ALL, D), BF16),
        scratch_shapes=[pltpu.VMEM((TM, D), BF16)],
        compiler_params=_cparams(("arbitrary",)),
        name=f"gmlp_mix_l{layer}",
    )(*src_args, norm_g.reshape(1, D), mod, mod, w_in, b_in.reshape(1, 2 * D), ln_g.reshape(1, D), w_s, b_s)


def _out_proj_kernel(*refs, n_src):
    src = refs[:n_src]
    a_ref, w_ref, gate_ref, g_ref, sh_ref, sc_ref, o_ref, h_ref = refs[n_src:]
    y = jnp.dot(a_ref[...], w_ref[...], preferred_element_type=F32)
    x1 = _token_tile(src) + gate_ref[...] * y
    o_ref[...] = x1
    h_ref[...] = _norm_mod(x1, g_ref[...], sh_ref[...], sc_ref[...]).astype(BF16)


def _out_proj(xs, a, w, mod, layer, n_tiles, ffn_norm_g):
    ka = a.shape[1]
    src_specs, src_args = _token_specs(xs)
    tile = pl.BlockSpec((TM, D), lambda i: (i, 0))
    return pl.pallas_call(
        functools.partial(_out_proj_kernel, n_src=len(src_args)),
        grid=(n_tiles,),
        in_specs=src_specs + [
            pl.BlockSpec((TM, ka), lambda i: (i, 0)),
            pl.BlockSpec((ka, D), lambda i: (0, 0)),
            _mod_spec(layer, 2),
            pl.BlockSpec((1, D), lambda i: (0, 0)),
            _mod_spec(layer, 3), _mod_spec(layer, 4),
        ],
        out_specs=[tile, tile],
        out_shape=[jax.ShapeDtypeStruct((T_ALL, D), F32), jax.ShapeDtypeStruct((T_ALL, D), BF16)],
        input_output_aliases={0: 0} if len(src_args) == 1 else {},
        compiler_params=_cparams(("arbitrary",)),
        name=f"out_proj_l{layer}",
    )(*src_args, a, w, mod, ffn_norm_g.reshape(1, D), mod, mod)


def _ffn_kernel(x_ref, h_ref, gate_ref, w1_ref, w2_ref, fg_ref, o_ref, *, n_f, final_norm):
    f = pl.program_id(1)

    @pl.when(f == 0)
    def _():
        o_ref[...] = jnp.zeros_like(o_ref)

    a = jnp.dot(h_ref[...], w1_ref[...], preferred_element_type=F32)
    a = jnp.square(jnp.maximum(a, 0.0)).astype(BF16)
    o_ref[...] += jnp.dot(a, w2_ref[...], preferred_element_type=F32)

    @pl.when(f == n_f - 1)
    def _():
        y = x_ref[...] + gate_ref[...] * o_ref[...]
        if final_norm:
            y = y * lax.rsqrt(jnp.mean(y * y, axis=-1, keepdims=True) + EPS) * fg_ref[...]
        o_ref[...] = y


def _ffn(xs, h, mod, layer, n_tiles, w1, w2, final_g, final_norm):
    tf = 1024
    n_f = D_FF // tf
    rows = T_LAT if final_norm else T_ALL
    tile = pl.BlockSpec((TM, D), lambda i, f: (i, 0))
    return pl.pallas_call(
        functools.partial(_ffn_kernel, n_f=n_f, final_norm=final_norm),
        grid=(n_tiles, n_f),
        in_specs=[
            tile, tile,
            _mod_spec(layer, 5),
            pl.BlockSpec((None, D, tf), lambda i, f: (layer, 0, f)),
            pl.BlockSpec((None, tf, D), lambda i, f: (layer, f, 0)),
            pl.BlockSpec((1, D), lambda i, f: (0, 0)),
        ],
        out_specs=tile,
        out_shape=jax.ShapeDtypeStruct((rows, D), F32),
        input_output_aliases={} if final_norm else {0: 0},
        compiler_params=_cparams(("arbitrary", "arbitrary")),
        name=f"ffn_l{layer}",
    )(xs, h, mod, w1, w2, final_g.reshape(1, D))


TB = 256
LOG2E = 1.4426950408889634
DIFF_Q_SCALE = DIFF_D ** -0.5 * LOG2E
MLA_Q_SCALE = MLA_QK ** -0.5 * LOG2E
DIFF_BAND = DIFF_H * DIFF_V
PROJ_SUB = 512


def _diff_proj_kernel(x_ref, g_ref, sh_ref, sc_ref, w_ref, cos_ref, sa_ref, sb_ref,
                      qt_ref, k_ref, vt_ref, h_ref):
    n = pl.program_id(1)

    @pl.when(n == 0)
    def _():
        h_ref[...] = _norm_mod(x_ref[...], g_ref[...], sh_ref[...], sc_ref[...]).astype(BF16)

    def sub_dot(s):
        return jnp.dot(h_ref[...], w_ref[:, s * PROJ_SUB:(s + 1) * PROJ_SUB], preferred_element_type=F32)

    def roped(y):
        cos, sa, sb = cos_ref[...], sa_ref[...], sb_ref[...]
        return jnp.concatenate([_rope(y[:, c * 128:(c + 1) * 128], cos, sa, sb, 32)
                                for c in range(PROJ_SUB // 128)], axis=1)

    def store_t(ref, s, y):
        for t in range(TM // TB):
            ref[t, s * PROJ_SUB:(s + 1) * PROJ_SUB, :] = y[t * TB:(t + 1) * TB, :].T.astype(BF16)

    @pl.when(n == 0)
    def _():
        for s in range(DIFF_BAND // PROJ_SUB):
            store_t(qt_ref, s, roped(sub_dot(s) * DIFF_Q_SCALE))

    @pl.when(n == 1)
    def _():
        for s in range(DIFF_BAND // PROJ_SUB):
            k_ref[:, s * PROJ_SUB:(s + 1) * PROJ_SUB] = roped(sub_dot(s)).astype(BF16)

    @pl.when(n == 2)
    def _():
        for s in range(DIFF_BAND // PROJ_SUB):
            store_t(vt_ref, s, sub_dot(s))


def _diff_proj(xs, mod, layer, norm_g, w_qkv, tabs):
    const2 = lambda i, n: (0, 0)
    tab_spec = pl.BlockSpec((TM, 128), lambda i, n: (_rope_tile(i), 0))
    t_spec = pl.BlockSpec((TM // TB, DIFF_BAND, TB), lambda i, n: (i, 0, 0))
    t_shape = jax.ShapeDtypeStruct((T_ALL // TB, DIFF_BAND, TB), BF16)
    return pl.pallas_call(
        _diff_proj_kernel,
        grid=(N_ALL_TILES, 3),
        in_specs=[
            pl.BlockSpec((TM, D), lambda i, n: (i, 0)),
            pl.BlockSpec((1, D), const2),
            _mod_spec(layer, 0), _mod_spec(layer, 1),
            pl.BlockSpec((D, DIFF_BAND), lambda i, n: (0, n)),
            tab_spec, tab_spec, tab_spec,
        ],
        out_specs=[t_spec, pl.BlockSpec((TM, DIFF_BAND), lambda i, n: (i, 0)), t_spec],
        out_shape=[t_shape, jax.ShapeDtypeStruct((T_ALL, DIFF_BAND), BF16), t_shape],
        scratch_shapes=[pltpu.VMEM((TM, D), BF16)],
        compiler_params=_cparams(("arbitrary", "arbitrary")),
        name=f"diff_proj_l{layer}",
    )(xs, norm_g.reshape(1, D), mod, mod, w_qkv, *tabs)


TQ = 1024
TK = 1024
SUM_ROWS = 16


def _scores(k, q_t):
    return jnp.dot(k, q_t, preferred_element_type=F32)


def _attn_update(s, m_blk, v_t_parts, state, first):
    m_ref, l_ref, acc_ref = state
    dv = acc_ref.shape[0]
    m_new = m_blk if first else jnp.maximum(m_ref[...], m_blk)
    pb = jnp.exp2(s - m_new).astype(BF16)
    ones = jnp.ones((SUM_ROWS, TB), BF16)
    pv = None
    for t, v_t in enumerate(v_t_parts):
        d = jnp.dot(jnp.concatenate([v_t, ones], axis=0), pb[t * TB:(t + 1) * TB, :],
                    preferred_element_type=F32)
        pv = d if pv is None else pv + d
    l_blk = pv[dv:dv + 1, :]
    if first:
        l_ref[...] = l_blk
        acc_ref[...] = pv[:dv, :]
    else:
        alpha = jnp.exp2(m_ref[...] - m_new)
        l_ref[...] = alpha * l_ref[...] + l_blk
        acc_ref[...] = alpha * acc_ref[...] + pv[:dv, :]
    m_ref[...] = m_new


def _attn_pipeline(score_fn, v_fn, head_scores, head_v, states, slot_a, slot_b, n):
    def stash(j, slot):
        for s, (s_ref, mb_ref) in zip(score_fn(j), slot):
            s_ref[...] = s
            mb_ref[...] = jnp.max(s, axis=0, keepdims=True)

    def update(j, slot):
        for (s_ref, mb_ref), v_parts, state in zip(slot, v_fn(j), states):
            _attn_update(s_ref[...], mb_ref[...], v_parts, state, False)

    stash(0, slot_a)
    for s, v_parts, state in zip(head_scores, head_v, states):
        _attn_update(s, jnp.max(s, axis=0, keepdims=True), v_parts, state, True)

    def pair(i, carry):
        j = 2 * i
        stash(j + 1, slot_b)
        update(j, slot_a)
        stash(j + 2, slot_a)
        update(j + 1, slot_b)
        return carry

    lax.fori_loop(0, n // 2 - 1, pair, 0)
    stash(n - 1, slot_b)
    update(n - 2, slot_a)
    update(n - 1, slot_b)


def _load_q_t(qt_ref, rows):
    return jnp.concatenate([qt_ref[t, rows, :] for t in range(qt_ref.shape[0])], axis=1)


def _lat_chunk_fns(k_ref, vt_ref, v_rows):
    k_fn = lambda j: k_ref[pl.ds(pl.multiple_of(j * TK, TK), TK), :]
    v_fn = lambda j: [[vt_ref[(TK // TB) * j + t, rows, :] for t in range(TK // TB)] for rows in v_rows]
    return k_fn, v_fn


def _slot_scratch(n_chain, tq):
    return [pltpu.VMEM((TK, tq), F32), pltpu.VMEM((1, tq), F32)] * (2 * n_chain)


def _split_slots(refs, n_chain):
    pairs = [(refs[2 * i], refs[2 * i + 1]) for i in range(2 * n_chain)]
    return pairs[:n_chain], pairs[n_chain:]


def _diff_attn_kernel(*refs, has_lat, lambda_init):
    if has_lat:
        qt_ref, kc_ref, vtc_ref, k_ref, vt_ref, lam_ref, sg_ref, o_ref = refs[:8]
        scratch = refs[8:]
    else:
        qt_ref, kc_ref, vtc_ref, lam_ref, sg_ref, o_ref = refs[:6]
        scratch = refs[6:]
    states = (scratch[0:3], scratch[3:6])
    halves = (slice(0, DIFF_D), slice(DIFF_D, 2 * DIFF_D))

    def scores_of(k):
        return [_scores(k[:, halves[idx]], _load_q_t(qt_ref, halves[idx])) for idx in range(2)]

    ctx_scores = scores_of(kc_ref[...])
    v_rows = [slice(0, DIFF_V)] * 2
    ctx_v = [[vtc_ref[0]]] * 2
    if has_lat:
        k_fn, v_fn = _lat_chunk_fns(k_ref, vt_ref, v_rows)
        slot_a, slot_b = _split_slots(scratch[6:], 2)
        _attn_pipeline(lambda j: scores_of(k_fn(j)), v_fn, ctx_scores, ctx_v, states,
                       slot_a, slot_b, SEQ // TK)
    else:
        for s, v_parts, state in zip(ctx_scores, ctx_v, states):
            _attn_update(s, jnp.max(s, axis=0, keepdims=True), v_parts, state, True)

    lam_v = lam_ref[...]
    lam = (jnp.exp(jnp.sum(lam_v[0:1] * lam_v[1:2], axis=-1, keepdims=True))
           - jnp.exp(jnp.sum(lam_v[2:3] * lam_v[3:4], axis=-1, keepdims=True)) + lambda_init)
    (_, l0, a0), (_, l1, a1) = states
    o_t = a0[...] * (1.0 / l0[...]) - lam * (a1[...] * (1.0 / l1[...]))
    o = o_t.T
    o = o * lax.rsqrt(jnp.mean(o * o, axis=-1, keepdims=True) + EPS) * sg_ref[...]
    o_ref[...] = (o * (1.0 - lambda_init)).astype(BF16)


def _diff_attn(q_t, k, v_t, lam_tab, subln_g, lambda_init, o_prev):
    has_lat = o_prev is None
    ctx_blk0 = T_LAT // CTX
    if has_lat:
        tq = TQ
        grid = (BATCH, DIFF_H, SEQ // tq)
        q_blk = lambda b, h, i: b * (SEQ // tq) + i
    else:
        tq = CTX
        grid = (BATCH, DIFF_H, 1)
        q_blk = lambda b, h, i: ctx_blk0 + b
    in_specs = [
        pl.BlockSpec((tq // TB, DIFF_V, TB), lambda b, h, i: (q_blk(b, h, i), h, 0)),
        pl.BlockSpec((CTX, DIFF_V), lambda b, h, i: (ctx_blk0 + b, h)),
        pl.BlockSpec((1, DIFF_V, TB), lambda b, h, i: (ctx_blk0 + b, h, 0)),
    ]
    args = [q_t, k, v_t]
    if has_lat:
        in_specs += [
            pl.BlockSpec((SEQ, DIFF_V), lambda b, h, i: (b, h)),
            pl.BlockSpec((SEQ // TB, DIFF_V, TB), lambda b, h, i: (b, h, 0)),
        ]
        args += [k, v_t]
    in_specs += [
        pl.BlockSpec((8, DIFF_D), lambda b, h, i: (0, 0)),
        pl.BlockSpec((1, DIFF_V), lambda b, h, i: (0, 0)),
    ]
    args += [lam_tab, subln_g.reshape(1, DIFF_V)]
    aliases = {}
    kern = functools.partial(_diff_attn_kernel, has_lat=has_lat, lambda_init=lambda_init)
    if not has_lat:
        in_specs.append(pl.BlockSpec(memory_space=pl.ANY))
        args.append(o_prev)
        aliases = {len(args) - 1: 0}
        base = kern
        kern = lambda *refs: base(*refs[:5], *refs[6:])
    return pl.pallas_call(
        kern,
        grid=grid,
        in_specs=in_specs,
        out_specs=pl.BlockSpec((tq, DIFF_V), lambda b, h, i: (q_blk(b, h, i), h)),
        out_shape=jax.ShapeDtypeStruct((T_ALL, DIFF_H * DIFF_V), BF16),
        input_output_aliases=aliases,
        scratch_shapes=([pltpu.VMEM((1, tq), F32), pltpu.VMEM((1, tq), F32), pltpu.VMEM((DIFF_V, tq), F32)] * 2
                        + (_slot_scratch(2, tq) if has_lat else [])),
        compiler_params=_cparams(("arbitrary",) * 3, ATTN_FLAGS),
        name="diff_attn_lat" if has_lat else "diff_attn_ctx",
    )(*args)


MLA_HPS = 2


def _mla_attn_kernel(qt_ref, kc_ref, vtc_ref, k_ref, vt_ref, o_ref, *scratch):
    states = [scratch[3 * i:3 * i + 3] for i in range(MLA_HPS)]
    qk = [slice(i * MLA_QK_PAD, (i + 1) * MLA_QK_PAD) for i in range(MLA_HPS)]
    v_rows = [slice(i * MLA_V, (i + 1) * MLA_V) for i in range(MLA_HPS)]

    def scores_of(k):
        return [_scores(k[:, qk[i]], _load_q_t(qt_ref, qk[i])) for i in range(MLA_HPS)]

    k_fn, v_fn = _lat_chunk_fns(k_ref, vt_ref, v_rows)
    slot_a, slot_b = _split_slots(scratch[3 * MLA_HPS:], MLA_HPS)
    _attn_pipeline(lambda j: scores_of(k_fn(j)), v_fn, scores_of(kc_ref[...]),
                   [[vtc_ref[0, rows, :]] for rows in v_rows], states, slot_a, slot_b, SEQ // TK)
    for i, (_, l_ref, acc_ref) in enumerate(states):
        o_ref[:, v_rows[i]] = (acc_ref[...] * (1.0 / l_ref[...])).T.astype(BF16)


def _mla_attn(q_t, k, v_t):
    ctx_blk0 = T_LAT // CTX
    qk_w, v_w = MLA_HPS * MLA_QK_PAD, MLA_HPS * MLA_V
    return pl.pallas_call(
        _mla_attn_kernel,
        grid=(BATCH, MLA_H // MLA_HPS, SEQ // TQ),
        in_specs=[
            pl.BlockSpec((TQ // TB, qk_w, TB), lambda b, h, i: (b * (SEQ // TQ) + i, h, 0)),
            pl.BlockSpec((CTX, qk_w), lambda b, h, i: (ctx_blk0 + b, h)),
            pl.BlockSpec((1, v_w, TB), lambda b, h, i: (ctx_blk0 + b, h, 0)),
            pl.BlockSpec((SEQ, qk_w), lambda b, h, i: (b, h)),
            pl.BlockSpec((SEQ // TB, v_w, TB), lambda b, h, i: (b, h, 0)),
        ],
        out_specs=pl.BlockSpec((TQ, v_w), lambda b, h, i: (b * (SEQ // TQ) + i, h)),
        out_shape=jax.ShapeDtypeStruct((T_LAT, MLA_H * MLA_V), BF16),
        scratch_shapes=([pltpu.VMEM((1, TQ), F32), pltpu.VMEM((1, TQ), F32), pltpu.VMEM((MLA_V, TQ), F32)]
                        * MLA_HPS + _slot_scratch(MLA_HPS, TQ)),
        compiler_params=_cparams(("arbitrary",) * 3, ATTN_FLAGS),
        name="mla_attn",
    )(q_t, k, v_t, k, v_t)


def _mla_proj_kernel(x_ref, g_ref, sh_ref, sc_ref, wd_ref, qg_ref, kvg_ref, wuq_ref, wukv_ref,
                     cos_ref, sa_ref, sb_ref, qt_ref, k_ref, vt_ref):
    h = _norm_mod(x_ref[...], g_ref[...], sh_ref[...], sc_ref[...]).astype(BF16)
    c = jnp.dot(h, wd_ref[...], preferred_element_type=F32)
    cq = c[:, :MLA_QR_PAD]
    ckv = c[:, MLA_QR_PAD:MLA_QR_PAD + MLA_KVR]
    kr = c[:, MLA_QR_PAD + MLA_KVR:]
    cqn = cq * lax.rsqrt(jnp.sum(cq * cq, axis=-1, keepdims=True) * (1.0 / MLA_QR) + EPS) * qg_ref[...]
    ckvn = ckv * lax.rsqrt(jnp.mean(ckv * ckv, axis=-1, keepdims=True) + EPS) * kvg_ref[...]
    cos, sa, sb = cos_ref[...], sa_ref[...], sb_ref[...]
    kr_rot = _rope(kr, cos, sa, sb, 16).astype(BF16)
    q = jnp.dot(cqn.astype(BF16), wuq_ref[...], preferred_element_type=F32) * MLA_Q_SCALE
    kv = jnp.dot(ckvn.astype(BF16), wukv_ref[...], preferred_element_type=F32)
    for hd in range(MLA_H):
        base = hd * MLA_QK_PAD
        qt_ref[0, base:base + 128, :] = q[:, base:base + 128].T.astype(BF16)
        qt_ref[0, base + 128:base + 256, :] = _rope(q[:, base + 128:base + 256], cos, sa, sb, 16).T.astype(BF16)
        k_ref[:, base:base + 128] = kv[:, hd * MLA_NOPE:(hd + 1) * MLA_NOPE].astype(BF16)
        k_ref[:, base + 128:base + 256] = kr_rot
        vcols = slice(MLA_H * MLA_NOPE + hd * MLA_V, MLA_H * MLA_NOPE + (hd + 1) * MLA_V)
        vt_ref[0, hd * MLA_V:(hd + 1) * MLA_V, :] = kv[:, vcols].T.astype(BF16)


def _mla_proj(xs, mod, layer, norm_g, w_d, q_g, kv_g, w_uq, w_ukv, tabs):
    tm = TB
    scale = TM // tm
    const2 = lambda i: (0, 0)
    mod_spec = lambda k: pl.BlockSpec((None, None, 1, D), lambda i: (layer, _mod_row(i // scale), 0, k))
    tab_spec = pl.BlockSpec((tm, 128), lambda i: (
        jnp.where(i < N_LAT_TILES * scale, i % (TILES_PER_BATCH * scale), TILES_PER_BATCH * scale), 0))
    n_q = MLA_H * MLA_QK_PAD
    return pl.pallas_call(
        _mla_proj_kernel,
        grid=(N_ALL_TILES * scale,),
        in_specs=[
            pl.BlockSpec((tm, D), lambda i: (i, 0)),
            pl.BlockSpec((1, D), const2),
            mod_spec(0), mod_spec(1),
            pl.BlockSpec((D, MLA_C_PAD), const2),
            pl.BlockSpec((1, MLA_QR_PAD), const2),
            pl.BlockSpec((1, MLA_KVR), const2),
            pl.BlockSpec((MLA_QR_PAD, n_q), const2),
            pl.BlockSpec((MLA_KVR, MLA_H * (MLA_NOPE + MLA_V)), const2),
            tab_spec, tab_spec, tab_spec,
        ],
        out_specs=[
            pl.BlockSpec((1, n_q, tm), lambda i: (i, 0, 0)),
            pl.BlockSpec((tm, n_q), lambda i: (i, 0)),
            pl.BlockSpec((1, MLA_H * MLA_V, tm), lambda i: (i, 0, 0)),
        ],
        out_shape=[
            jax.ShapeDtypeStruct((T_ALL // tm, n_q, tm), BF16),
            jax.ShapeDtypeStruct((T_ALL, n_q), BF16),
            jax.ShapeDtypeStruct((T_ALL // tm, MLA_H * MLA_V, tm), BF16),
        ],
        compiler_params=_cparams(("arbitrary",)),
        name=f"mla_proj_l{layer}",
    )(xs, norm_g.reshape(1, D), mod, mod, w_d, q_g, kv_g, w_uq, w_ukv, *tabs)


def _rope_tables(rot_dim):
    a = rot_dim // 2
    ch = a // 2
    n_rows = SEQ // GRID_W
    inv = ROPE_BASE ** (-jnp.arange(0, a, 2, dtype=F32) / a)

    def axis_angles(n):
        ang = jnp.arange(n, dtype=jnp.int32).astype(F32)[:, None] * inv[None, :]
        return jnp.concatenate([ang, ang], axis=-1)

    def grid(fn):
        by_row = jnp.broadcast_to(fn(axis_angles(n_rows))[:, None, :], (n_rows, GRID_W, a))
        by_col = jnp.broadcast_to(fn(axis_angles(GRID_W))[None, :, :], (n_rows, GRID_W, a))
        return jnp.concatenate([by_row, by_col], axis=-1).reshape(SEQ, rot_dim)

    cos, sin = grid(jnp.cos), grid(jnp.sin)
    lane = jnp.arange(rot_dim)
    first = (lane % a) < ch
    sin_a = jnp.where(first[None, :], -sin, 0.0)
    sin_b = jnp.where(first[None, :], 0.0, sin)
    pad = 128 - rot_dim
    cos = jnp.pad(cos, ((0, TM), (0, pad)), constant_values=1.0)
    cos = cos.at[SEQ:, :].set(1.0)
    sin_a = jnp.pad(sin_a, ((0, TM), (0, pad)))
    sin_b = jnp.pad(sin_b, ((0, TM), (0, pad)))
    return cos, sin_a, sin_b


def kernel(x, c, ctx, c_ctx, w_mod, b_mod, norm1_g, norm2_g, w_ff1, w_ff2, a_w_in, a_b_in, a_ln_g, a_w_s, a_b_s, a_w_out, b_w_qkv, b_lam_q1, b_lam_k1, b_lam_q2, b_lam_k2, b_subln_g, b_w_o, c_w_dqkv, c_q_norm_g, c_w_uq, c_kv_norm_g, c_w_ukv, c_w_o, final_g):
    xs = (x.reshape(T_LAT, D), ctx.reshape(T_CTX, D))
    cc = jnp.concatenate([c, c_ctx[None, :], jnp.zeros((8 - BATCH - 1, D), F32)], axis=0)
    mod = _modulation(cc, w_mod, b_mod).reshape(DEPTH, 8, 1, N_MOD * D)

    tabs_b = _rope_tables(DIFF_D)
    tabs_c = _rope_tables(MLA_ROPE)
    w_ff1_b, w_ff2_b = w_ff1.astype(BF16), w_ff2.astype(BF16)

    out = None
    for l in range(DEPTH):
        kind, idx = l % 3, l // 3
        ctx_later = any(j % 3 != 0 for j in range(l + 1, DEPTH))
        last = l == DEPTH - 1
        n_tiles = N_ALL_TILES if ctx_later else N_LAT_TILES
        if kind == 0:
            b_s = jnp.broadcast_to(a_b_s[idx][:, :, None], (GROUPS, CHUNK, CHUNK))
            t = _gmlp_mix(xs, mod, l, n_tiles, norm1_g[l], a_w_in[idx].astype(BF16), a_b_in[idx],
                          a_ln_g[idx], a_w_s[idx].astype(BF16), b_s)
            xs, h2 = _out_proj(xs, t, a_w_out[idx].astype(BF16), mod, l, n_tiles, norm2_g[l])
        elif kind == 1:
            lambda_init = 0.8 - 0.6 * math.exp(-0.3 * l)
            q_t, k, v_t = _diff_proj(xs, mod, l, norm1_g[l], b_w_qkv[idx].astype(BF16), tabs_b)
            lam_tab = jnp.concatenate([b_lam_q1[idx][None], b_lam_k1[idx][None], b_lam_q2[idx][None],
                                       b_lam_k2[idx][None], jnp.zeros((4, DIFF_D), F32)], axis=0)
            o = _diff_attn(q_t, k, v_t, lam_tab, b_subln_g[idx], lambda_init, None)
            if ctx_later:
                o = _diff_attn(q_t, k, v_t, lam_tab, b_subln_g[idx], lambda_init, o)
            xs, h2 = _out_proj(xs, o, b_w_o[idx].astype(BF16), mod, l, n_tiles, norm2_g[l])
        else:
            w_d = c_w_dqkv[idx]
            w_d = jnp.concatenate([
                jnp.pad(w_d[:, :MLA_QR], ((0, 0), (0, MLA_QR_PAD - MLA_QR))),
                w_d[:, MLA_QR:MLA_QR + MLA_KVR],
                jnp.pad(w_d[:, MLA_QR + MLA_KVR:], ((0, 0), (0, 128 - MLA_ROPE)))], axis=1).astype(BF16)
            w_uq = jnp.pad(c_w_uq[idx].reshape(MLA_QR, MLA_H, MLA_QK),
                           ((0, MLA_QR_PAD - MLA_QR), (0, 0), (0, MLA_QK_PAD - MLA_QK)))
            w_uq = w_uq.reshape(MLA_QR_PAD, MLA_H * MLA_QK_PAD).astype(BF16)
            w_ukv = c_w_ukv[idx].reshape(MLA_KVR, MLA_H, MLA_NOPE + MLA_V)
            w_ukv = jnp.concatenate([w_ukv[:, :, :MLA_NOPE].reshape(MLA_KVR, MLA_H * MLA_NOPE),
                                     w_ukv[:, :, MLA_NOPE:].reshape(MLA_KVR, MLA_H * MLA_V)],
                                    axis=1).astype(BF16)
            q_g = jnp.pad(c_q_norm_g[idx], (0, MLA_QR_PAD - MLA_QR)).reshape(1, MLA_QR_PAD)
            q_t, k, v_t = _mla_proj(xs, mod, l, norm1_g[l], w_d, q_g, c_kv_norm_g[idx].reshape(1, MLA_KVR),
                                    w_uq, w_ukv, tabs_c)
            o = _mla_attn(q_t, k, v_t)
            xs, h2 = _out_proj(xs, o, c_w_o[idx].astype(BF16), mod, l, N_LAT_TILES, norm2_g[l])
        if ctx_later and kind == 2:
            raise NotImplementedError("context update after an MLA layer is not needed at this depth")
        res = _ffn(xs, h2, mod, l, n_tiles, w_ff1_b, w_ff2_b, final_g, last)
        if last:
            out = res
        else:
            xs = res
    return out.reshape(BATCH, SEQ, D)
```

```python
import functools
import math

import jax
import jax.numpy as jnp
from jax import lax
from jax.experimental import pallas as pl
from jax.experimental.pallas import tpu as pltpu

D = 2048
BATCH = 2
SEQ = 8192
DEPTH = 4
GRID_W = 64
CTX = 256
N_MOD = 6
D_FF = 4 * D
EPS = 1e-6
ROPE_BASE = 10000.0

CHUNK = 128
GROUPS = 16
GDIM = 128

DIFF_H = 8
DIFF_D = 128
DIFF_V = 256

MLA_H = 16
MLA_QR = 448
MLA_QR_PAD = 512
MLA_KVR = 512
MLA_NOPE = 128
MLA_ROPE = 64
MLA_V = 128
MLA_QK = MLA_NOPE + MLA_ROPE
MLA_QK_PAD = 256
MLA_C_PAD = MLA_QR_PAD + MLA_KVR + 128

T_LAT = BATCH * SEQ
T_CTX = BATCH * CTX
T_ALL = T_LAT + T_CTX

TM = 512
TILES_PER_BATCH = SEQ // TM
N_LAT_TILES = T_LAT // TM
N_ALL_TILES = T_ALL // TM
ROPE_ID_TILE = SEQ // TM

VMEM_LIMIT = 56 * 1024 * 1024

BF16 = jnp.bfloat16
F32 = jnp.float32


def _cparams(sem, flags=None):
    return pltpu.CompilerParams(dimension_semantics=sem, vmem_limit_bytes=VMEM_LIMIT, flags=flags)


ATTN_FLAGS = None


def _mod_row(i):
    return jnp.minimum(i // TILES_PER_BATCH, 2)


def _mod_spec(layer, k):
    return pl.BlockSpec((None, None, 1, D), lambda i, *_: (layer, _mod_row(i), 0, k))


def _rope_tile(i):
    return jnp.where(i < N_LAT_TILES, i % TILES_PER_BATCH, ROPE_ID_TILE)


def _norm_mod(x, g, sh, sc):
    y = x * lax.rsqrt(jnp.mean(x * x, axis=-1, keepdims=True) + EPS)
    y = y * g
    return y * (1.0 + sc) + sh


def _token_specs(xs):
    if isinstance(xs, tuple):
        width = xs[0].shape[1]
        return ([pl.BlockSpec((TM, width), lambda i, *_: (jnp.minimum(i, N_LAT_TILES - 1), 0)),
                 pl.BlockSpec((TM, width), lambda i, *_: (jnp.maximum(i - N_LAT_TILES, 0), 0))], list(xs))
    return [pl.BlockSpec((TM, xs.shape[1]), lambda i, *_: (i, 0))], [xs]


def _token_tile(src_refs):
    if len(src_refs) == 2:
        return jnp.where(pl.program_id(0) < N_LAT_TILES, src_refs[0][...], src_refs[1][...])
    return src_refs[0][...]


def _rope(x, cos, sin_a, sin_b, ch):
    return (x * cos + pltpu.roll(x, 128 - ch, 1) * sin_a + pltpu.roll(x, ch, 1) * sin_b)


def _mod_kernel(c_ref, w_ref, b_ref, o_ref):
    c = c_ref[...]
    s = (c * jax.nn.sigmoid(c)).astype(BF16)
    o_ref[...] = jnp.dot(s, w_ref[...].astype(BF16), preferred_element_type=F32) + b_ref[...]


def _modulation(cc, w_mod, b_mod):
    tn = 1024
    n = N_MOD * D
    return pl.pallas_call(
        _mod_kernel,
        grid=(DEPTH, n // tn),
        in_specs=[
            pl.BlockSpec((8, D), lambda l, j: (0, 0)),
            pl.BlockSpec((None, D, tn), lambda l, j: (l, 0, j)),
            pl.BlockSpec((None, 1, tn), lambda l, j: (l, 0, j)),
        ],
        out_specs=pl.BlockSpec((None, 8, tn), lambda l, j: (l, 0, j)),
        out_shape=jax.ShapeDtypeStruct((DEPTH, 8, n), F32),
        compiler_params=_cparams(("arbitrary", "arbitrary")),
        name="modulation",
    )(cc, w_mod, b_mod.reshape(DEPTH, 1, n))


def _gelu_tanh(x):
    return 0.5 * x * (1.0 + jnp.tanh(math.sqrt(2.0 / math.pi) * (x + 0.044715 * (x * x * x))))


def _gmlp_kernel(*refs, n_src):
    src = refs[:n_src]
    g_ref, sh_ref, sc_ref, win_ref, bin_ref, lng_ref, ws_ref, bs_ref, o_ref, vn_ref = refs[n_src:]
    nchunk = TM // CHUNK
    h = _norm_mod(_token_tile(src), g_ref[...], sh_ref[...], sc_ref[...]).astype(BF16)
    zv = _gelu_tanh(jnp.dot(h, win_ref[:, D:], preferred_element_type=F32) + bin_ref[:, D:])
    mu = jnp.mean(zv, axis=-1, keepdims=True)
    zc = zv - mu
    vn = zc * lax.rsqrt(jnp.mean(zc * zc, axis=-1, keepdims=True) + EPS) * lng_ref[...]
    vn_ref[...] = vn.astype(BF16)
    ublk = 512
    zus = [_gelu_tanh(jnp.dot(h, win_ref[:, ub * ublk:(ub + 1) * ublk], preferred_element_type=F32)
                      + bin_ref[:, ub * ublk:(ub + 1) * ublk]) for ub in range(D // ublk)]
    for ub, zu in enumerate(zus):
        for gg in range(ublk // GDIM):
            grp = ub * (ublk // GDIM) + gg
            cols = slice(grp * GDIM, (grp + 1) * GDIM)
            rhs = jnp.concatenate([vn_ref[c * CHUNK:(c + 1) * CHUNK, cols] for c in range(nchunk)], axis=1)
            sv = jnp.dot(ws_ref[grp], rhs, preferred_element_type=F32)
            for c in range(nchunk):
                svc = sv[:, c * CHUNK:(c + 1) * CHUNK] + bs_ref[grp]
                u = zu[c * CHUNK:(c + 1) * CHUNK, gg * GDIM:(gg + 1) * GDIM]
                o_ref[c * CHUNK:(c + 1) * CHUNK, cols] = (u * svc).astype(BF16)


def _gmlp_mix(xs, mod, layer, n_tiles, norm_g, w_in, b_in, ln_g, w_s, b_s):
    const2 = lambda i: (0, 0)
    src_specs, src_args = _token_specs(xs)
    return pl.pallas_call(
        functools.partial(_gmlp_kernel, n_src=len(src_args)),
        grid=(n_tiles,),
        in_specs=src_specs + [
            pl.BlockSpec((1, D), const2),
            _mod_spec(layer, 0), _mod_spec(layer, 1),
            pl.BlockSpec((D, 2 * D), const2),
            pl.BlockSpec((1, 2 * D), const2),
            pl.BlockSpec((1, D), const2),
            pl.BlockSpec((GROUPS, CHUNK, CHUNK), lambda i: (0, 0, 0)),
            pl.BlockSpec((GROUPS, CHUNK, CHUNK), lambda i: (0, 0, 0)),
        ],
        out_specs=pl.BlockSpec((TM, D), lambda i: (i, 0)),
        out_shape=jax.ShapeDtypeStruct((n_tiles * TM, D), BF16),
        scratch_shapes=[pltpu.VMEM((TM, D), BF16)],
        compiler_params=_cparams(("arbitrary",)),
        name=f"gmlp_mix_l{layer}",
    )(*src_args, norm_g.reshape(1, D), mod, mod, w_in, b_in.reshape(1, 2 * D), ln_g.reshape(1, D), w_s, b_s)


def _out_proj_kernel(*refs, n_src, n_act):
    src = refs[:n_src]
    act = refs[n_src:n_src + n_act]
    w_ref, gate_ref, g_ref, sh_ref, sc_ref, o_ref, h_ref = refs[n_src + n_act:]
    y = jnp.dot(_token_tile(act), w_ref[...], preferred_element_type=F32)
    x1 = _token_tile(src) + gate_ref[...] * y
    o_ref[...] = x1
    h_ref[...] = _norm_mod(x1, g_ref[...], sh_ref[...], sc_ref[...]).astype(BF16)


def _out_proj(xs, a, w, mod, layer, n_tiles, ffn_norm_g):
    src_specs, src_args = _token_specs(xs)
    act_specs, act_args = _token_specs(a)
    tile = pl.BlockSpec((TM, D), lambda i: (i, 0))
    return pl.pallas_call(
        functools.partial(_out_proj_kernel, n_src=len(src_args), n_act=len(act_args)),
        grid=(n_tiles,),
        in_specs=src_specs + act_specs + [
            pl.BlockSpec((w.shape[0], D), lambda i: (0, 0)),
            _mod_spec(layer, 2),
            pl.BlockSpec((1, D), lambda i: (0, 0)),
            _mod_spec(layer, 3), _mod_spec(layer, 4),
        ],
        out_specs=[tile, tile],
        out_shape=[jax.ShapeDtypeStruct((T_ALL, D), F32), jax.ShapeDtypeStruct((n_tiles * TM, D), BF16)],
        input_output_aliases={0: 0} if len(src_args) == 1 else {},
        compiler_params=_cparams(("arbitrary",)),
        name=f"out_proj_l{layer}",
    )(*src_args, *act_args, w, mod, ffn_norm_g.reshape(1, D), mod, mod)


def _ffn_kernel(x_ref, h_ref, gate_ref, w1_ref, w2_ref, fg_ref, o_ref, *, n_f, final_norm):
    f = pl.program_id(1)

    @pl.when(f == 0)
    def _():
        o_ref[...] = jnp.zeros_like(o_ref)

    a = jnp.dot(h_ref[...], w1_ref[...], preferred_element_type=F32)
    a = jnp.square(jnp.maximum(a, 0.0)).astype(BF16)
    o_ref[...] += jnp.dot(a, w2_ref[...], preferred_element_type=F32)

    @pl.when(f == n_f - 1)
    def _():
        y = x_ref[...] + gate_ref[...] * o_ref[...]
        if final_norm:
            y = y * lax.rsqrt(jnp.mean(y * y, axis=-1, keepdims=True) + EPS) * fg_ref[...]
        o_ref[...] = y


def _ffn(xs, h, mod, layer, n_tiles, w1, w2, final_g, final_norm):
    tf = 1024
    n_f = D_FF // tf
    rows = T_LAT if final_norm else T_ALL
    tile = pl.BlockSpec((TM, D), lambda i, f: (i, 0))
    return pl.pallas_call(
        functools.partial(_ffn_kernel, n_f=n_f, final_norm=final_norm),
        grid=(n_tiles, n_f),
        in_specs=[
            tile, tile,
            _mod_spec(layer, 5),
            pl.BlockSpec((None, D, tf), lambda i, f: (layer, 0, f)),
            pl.BlockSpec((None, tf, D), lambda i, f: (layer, f, 0)),
            pl.BlockSpec((1, D), lambda i, f: (0, 0)),
        ],
        out_specs=tile,
        out_shape=jax.ShapeDtypeStruct((rows, D), F32),
        input_output_aliases={} if final_norm else {0: 0},
        compiler_params=_cparams(("arbitrary", "arbitrary")),
        name=f"ffn_l{layer}",
    )(xs, h, mod, w1, w2, final_g.reshape(1, D))


TB = 256
LOG2E = 1.4426950408889634
DIFF_Q_SCALE = DIFF_D ** -0.5 * LOG2E
MLA_Q_SCALE = MLA_QK ** -0.5 * LOG2E
DIFF_BAND = DIFF_H * DIFF_V
PROJ_SUB = 512


def _diff_proj_kernel(x_ref, g_ref, sh_ref, sc_ref, w_ref, cos_ref, sa_ref, sb_ref,
                      qt_ref, k_ref, vt_ref, h_ref):
    n = pl.program_id(1)

    @pl.when(n == 0)
    def _():
        h_ref[...] = _norm_mod(x_ref[...], g_ref[...], sh_ref[...], sc_ref[...]).astype(BF16)

    def sub_dot(s):
        return jnp.dot(h_ref[...], w_ref[:, s * PROJ_SUB:(s + 1) * PROJ_SUB], preferred_element_type=F32)

    def roped(y):
        cos, sa, sb = cos_ref[...], sa_ref[...], sb_ref[...]
        return jnp.concatenate([_rope(y[:, c * 128:(c + 1) * 128], cos, sa, sb, 32)
                                for c in range(PROJ_SUB // 128)], axis=1)

    def store_t(ref, s, y):
        for t in range(TM // TB):
            ref[t, s * PROJ_SUB:(s + 1) * PROJ_SUB, :] = y[t * TB:(t + 1) * TB, :].T.astype(BF16)

    @pl.when(n == 0)
    def _():
        for s in range(DIFF_BAND // PROJ_SUB):
            store_t(qt_ref, s, roped(sub_dot(s) * DIFF_Q_SCALE))

    @pl.when(n == 1)
    def _():
        for s in range(DIFF_BAND // PROJ_SUB):
            k_ref[:, s * PROJ_SUB:(s + 1) * PROJ_SUB] = roped(sub_dot(s)).astype(BF16)

    @pl.when(n == 2)
    def _():
        for s in range(DIFF_BAND // PROJ_SUB):
            store_t(vt_ref, s, sub_dot(s))


def _diff_proj(xs, mod, layer, norm_g, w_qkv, tabs):
    const2 = lambda i, n: (0, 0)
    tab_spec = pl.BlockSpec((TM, 128), lambda i, n: (_rope_tile(i), 0))
    t_spec = pl.BlockSpec((TM // TB, DIFF_BAND, TB), lambda i, n: (i, 0, 0))
    t_shape = jax.ShapeDtypeStruct((T_ALL // TB, DIFF_BAND, TB), BF16)
    return pl.pallas_call(
        _diff_proj_kernel,
        grid=(N_ALL_TILES, 3),
        in_specs=[
            pl.BlockSpec((TM, D), lambda i, n: (i, 0)),
            pl.BlockSpec((1, D), const2),
            _mod_spec(layer, 0), _mod_spec(layer, 1),
            pl.BlockSpec((D, DIFF_BAND), lambda i, n: (0, n)),
            tab_spec, tab_spec, tab_spec,
        ],
        out_specs=[t_spec, pl.BlockSpec((TM, DIFF_BAND), lambda i, n: (i, 0)), t_spec],
        out_shape=[t_shape, jax.ShapeDtypeStruct((T_ALL, DIFF_BAND), BF16), t_shape],
        scratch_shapes=[pltpu.VMEM((TM, D), BF16)],
        compiler_params=_cparams(("arbitrary", "arbitrary")),
        name=f"diff_proj_l{layer}",
    )(xs, norm_g.reshape(1, D), mod, mod, w_qkv, *tabs)


TQ = 1024
TK = 1024
SUM_ROWS = 16


def _scores(k, q_t):
    return jnp.dot(k, q_t, preferred_element_type=F32)


def _attn_update(s, m_blk, v_t_parts, state, first):
    m_ref, l_ref, acc_ref = state
    dv = acc_ref.shape[0]
    m_new = m_blk if first else jnp.maximum(m_ref[...], m_blk)
    pb = jnp.exp2(s - m_new).astype(BF16)
    ones = jnp.ones((SUM_ROWS, TB), BF16)
    pv = None
    for t, v_t in enumerate(v_t_parts):
        d = jnp.dot(jnp.concatenate([v_t, ones], axis=0), pb[t * TB:(t + 1) * TB, :],
                    preferred_element_type=F32)
        pv = d if pv is None else pv + d
    l_blk = pv[dv:dv + 1, :]
    if first:
        l_ref[...] = l_blk
        acc_ref[...] = pv[:dv, :]
    else:
        alpha = jnp.exp2(m_ref[...] - m_new)
        l_ref[...] = alpha * l_ref[...] + l_blk
        acc_ref[...] = alpha * acc_ref[...] + pv[:dv, :]
    m_ref[...] = m_new


def _attn_pipeline(score_fn, v_fn, head_scores, head_v, states, slot_a, slot_b, n):
    def stash(j, slot):
        for s, (s_ref, mb_ref) in zip(score_fn(j), slot):
            s_ref[...] = s
            mb_ref[...] = jnp.max(s, axis=0, keepdims=True)

    def update(j, slot):
        for (s_ref, mb_ref), v_parts, state in zip(slot, v_fn(j), states):
            _attn_update(s_ref[...], mb_ref[...], v_parts, state, False)

    stash(0, slot_a)
    for s, v_parts, state in zip(head_scores, head_v, states):
        _attn_update(s, jnp.max(s, axis=0, keepdims=True), v_parts, state, True)

    def pair(i, carry):
        j = 2 * i
        stash(j + 1, slot_b)
        update(j, slot_a)
        stash(j + 2, slot_a)
        update(j + 1, slot_b)
        return carry

    lax.fori_loop(0, n // 2 - 1, pair, 0)
    stash(n - 1, slot_b)
    update(n - 2, slot_a)
    update(n - 1, slot_b)


def _load_q_t(qt_ref, rows):
    return jnp.concatenate([qt_ref[t, rows, :] for t in range(qt_ref.shape[0])], axis=1)


def _lat_chunk_fns(k_ref, vt_ref, v_rows):
    k_fn = lambda j: k_ref[pl.ds(pl.multiple_of(j * TK, TK), TK), :]
    v_fn = lambda j: [[vt_ref[(TK // TB) * j + t, rows, :] for t in range(TK // TB)] for rows in v_rows]
    return k_fn, v_fn


def _slot_scratch(n_chain, tq):
    return [pltpu.VMEM((TK, tq), F32), pltpu.VMEM((1, tq), F32)] * (2 * n_chain)


def _split_slots(refs, n_chain):
    pairs = [(refs[2 * i], refs[2 * i + 1]) for i in range(2 * n_chain)]
    return pairs[:n_chain], pairs[n_chain:]


def _diff_attn_kernel(*refs, has_lat, lambda_init):
    if has_lat:
        qt_ref, kc_ref, vtc_ref, k_ref, vt_ref, lam_ref, sg_ref, o_ref = refs[:8]
        scratch = refs[8:]
    else:
        qt_ref, kc_ref, vtc_ref, lam_ref, sg_ref, o_ref = refs[:6]
        scratch = refs[6:]
    states = (scratch[0:3], scratch[3:6])
    halves = (slice(0, DIFF_D), slice(DIFF_D, 2 * DIFF_D))

    def scores_of(k):
        return [_scores(k[:, halves[idx]], _load_q_t(qt_ref, halves[idx])) for idx in range(2)]

    ctx_scores = scores_of(kc_ref[...])
    v_rows = [slice(0, DIFF_V)] * 2
    ctx_v = [[vtc_ref[0]]] * 2
    if has_lat:
        k_fn, v_fn = _lat_chunk_fns(k_ref, vt_ref, v_rows)
        slot_a, slot_b = _split_slots(scratch[6:], 2)
        _attn_pipeline(lambda j: scores_of(k_fn(j)), v_fn, ctx_scores, ctx_v, states,
                       slot_a, slot_b, SEQ // TK)
    else:
        for s, v_parts, state in zip(ctx_scores, ctx_v, states):
            _attn_update(s, jnp.max(s, axis=0, keepdims=True), v_parts, state, True)

    lam_v = lam_ref[...]
    lam = (jnp.exp(jnp.sum(lam_v[0:1] * lam_v[1:2], axis=-1, keepdims=True))
           - jnp.exp(jnp.sum(lam_v[2:3] * lam_v[3:4], axis=-1, keepdims=True)) + lambda_init)
    (_, l0, a0), (_, l1, a1) = states
    o_t = a0[...] * (1.0 / l0[...]) - lam * (a1[...] * (1.0 / l1[...]))
    o = o_t.T
    o = o * lax.rsqrt(jnp.mean(o * o, axis=-1, keepdims=True) + EPS) * sg_ref[...]
    o_ref[...] = (o * (1.0 - lambda_init)).astype(BF16)


def _diff_attn(q_t, k, v_t, lam_tab, subln_g, lambda_init, has_lat):
    ctx_blk0 = T_LAT // CTX
    if has_lat:
        tq = TQ
        grid = (BATCH, DIFF_H, SEQ // tq)
        q_blk = lambda b, h, i: b * (SEQ // tq) + i
        o_blk = q_blk
    else:
        tq = CTX
        grid = (BATCH, DIFF_H, 1)
        q_blk = lambda b, h, i: ctx_blk0 + b
        o_blk = lambda b, h, i: b
    in_specs = [
        pl.BlockSpec((tq // TB, DIFF_V, TB), lambda b, h, i: (q_blk(b, h, i), h, 0)),
        pl.BlockSpec((CTX, DIFF_V), lambda b, h, i: (ctx_blk0 + b, h)),
        pl.BlockSpec((1, DIFF_V, TB), lambda b, h, i: (ctx_blk0 + b, h, 0)),
    ]
    args = [q_t, k, v_t]
    if has_lat:
        in_specs += [
            pl.BlockSpec((SEQ, DIFF_V), lambda b, h, i: (b, h)),
            pl.BlockSpec((SEQ // TB, DIFF_V, TB), lambda b, h, i: (b, h, 0)),
        ]
        args += [k, v_t]
    in_specs += [
        pl.BlockSpec((8, DIFF_D), lambda b, h, i: (0, 0)),
        pl.BlockSpec((1, DIFF_V), lambda b, h, i: (0, 0)),
    ]
    args += [lam_tab, subln_g.reshape(1, DIFF_V)]
    return pl.pallas_call(
        functools.partial(_diff_attn_kernel, has_lat=has_lat, lambda_init=lambda_init),
        grid=grid,
        in_specs=in_specs,
        out_specs=pl.BlockSpec((tq, DIFF_V), lambda b, h, i: (o_blk(b, h, i), h)),
        out_shape=jax.ShapeDtypeStruct((T_LAT if has_lat else T_CTX, DIFF_H * DIFF_V), BF16),
        scratch_shapes=([pltpu.VMEM((1, tq), F32), pltpu.VMEM((1, tq), F32), pltpu.VMEM((DIFF_V, tq), F32)] * 2
                        + (_slot_scratch(2, tq) if has_lat else [])),
        compiler_params=_cparams(("arbitrary",) * 3, ATTN_FLAGS),
        name="diff_attn_lat" if has_lat else "diff_attn_ctx",
    )(*args)


MLA_HPS = 2


def _mla_attn_kernel(qt_ref, kc_ref, vtc_ref, k_ref, vt_ref, o_ref, *scratch):
    states = [scratch[3 * i:3 * i + 3] for i in range(MLA_HPS)]
    qk = [slice(i * MLA_QK_PAD, (i + 1) * MLA_QK_PAD) for i in range(MLA_HPS)]
    v_rows = [slice(i * MLA_V, (i + 1) * MLA_V) for i in range(MLA_HPS)]

    def scores_of(k):
        return [_scores(k[:, qk[i]], _load_q_t(qt_ref, qk[i])) for i in range(MLA_HPS)]

    k_fn, v_fn = _lat_chunk_fns(k_ref, vt_ref, v_rows)
    slot_a, slot_b = _split_slots(scratch[3 * MLA_HPS:], MLA_HPS)
    _attn_pipeline(lambda j: scores_of(k_fn(j)), v_fn, scores_of(kc_ref[...]),
                   [[vtc_ref[0, rows, :]] for rows in v_rows], states, slot_a, slot_b, SEQ // TK)
    for i, (_, l_ref, acc_ref) in enumerate(states):
        o_ref[:, v_rows[i]] = (acc_ref[...] * (1.0 / l_ref[...])).T.astype(BF16)


def _mla_attn(q_t, k, v_t):
    ctx_blk0 = T_LAT // CTX
    qk_w, v_w = MLA_HPS * MLA_QK_PAD, MLA_HPS * MLA_V
    return pl.pallas_call(
        _mla_attn_kernel,
        grid=(BATCH, MLA_H // MLA_HPS, SEQ // TQ),
        in_specs=[
            pl.BlockSpec((TQ // TB, qk_w, TB), lambda b, h, i: (b * (SEQ // TQ) + i, h, 0)),
            pl.BlockSpec((CTX, qk_w), lambda b, h, i: (ctx_blk0 + b, h)),
            pl.BlockSpec((1, v_w, TB), lambda b, h, i: (ctx_blk0 + b, h, 0)),
            pl.BlockSpec((SEQ, qk_w), lambda b, h, i: (b, h)),
            pl.BlockSpec((SEQ // TB, v_w, TB), lambda b, h, i: (b, h, 0)),
        ],
        out_specs=pl.BlockSpec((TQ, v_w), lambda b, h, i: (b * (SEQ // TQ) + i, h)),
        out_shape=jax.ShapeDtypeStruct((T_LAT, MLA_H * MLA_V), BF16),
        scratch_shapes=([pltpu.VMEM((1, TQ), F32), pltpu.VMEM((1, TQ), F32), pltpu.VMEM((MLA_V, TQ), F32)]
                        * MLA_HPS + _slot_scratch(MLA_HPS, TQ)),
        compiler_params=_cparams(("arbitrary",) * 3, ATTN_FLAGS),
        name="mla_attn",
    )(q_t, k, v_t, k, v_t)


def _mla_proj_kernel(x_ref, g_ref, sh_ref, sc_ref, wd_ref, qg_ref, kvg_ref, wuq_ref, wukv_ref,
                     cos_ref, sa_ref, sb_ref, qt_ref, k_ref, vt_ref):
    h = _norm_mod(x_ref[...], g_ref[...], sh_ref[...], sc_ref[...]).astype(BF16)
    c = jnp.dot(h, wd_ref[...], preferred_element_type=F32)
    cq = c[:, :MLA_QR_PAD]
    ckv = c[:, MLA_QR_PAD:MLA_QR_PAD + MLA_KVR]
    kr = c[:, MLA_QR_PAD + MLA_KVR:]
    cqn = cq * lax.rsqrt(jnp.sum(cq * cq, axis=-1, keepdims=True) * (1.0 / MLA_QR) + EPS) * qg_ref[...]
    ckvn = ckv * lax.rsqrt(jnp.mean(ckv * ckv, axis=-1, keepdims=True) + EPS) * kvg_ref[...]
    cos, sa, sb = cos_ref[...], sa_ref[...], sb_ref[...]
    kr_rot = _rope(kr, cos, sa, sb, 16).astype(BF16)
    q = jnp.dot(cqn.astype(BF16), wuq_ref[...], preferred_element_type=F32) * MLA_Q_SCALE
    kv = jnp.dot(ckvn.astype(BF16), wukv_ref[...], preferred_element_type=F32)
    for hd in range(MLA_H):
        base = hd * MLA_QK_PAD
        qt_ref[0, base:base + 128, :] = q[:, base:base + 128].T.astype(BF16)
        qt_ref[0, base + 128:base + 256, :] = _rope(q[:, base + 128:base + 256], cos, sa, sb, 16).T.astype(BF16)
        k_ref[:, base:base + 128] = kv[:, hd * MLA_NOPE:(hd + 1) * MLA_NOPE].astype(BF16)
        k_ref[:, base + 128:base + 256] = kr_rot
        vcols = slice(MLA_H * MLA_NOPE + hd * MLA_V, MLA_H * MLA_NOPE + (hd + 1) * MLA_V)
        vt_ref[0, hd * MLA_V:(hd + 1) * MLA_V, :] = kv[:, vcols].T.astype(BF16)


def _mla_proj(xs, mod, layer, norm_g, w_d, q_g, kv_g, w_uq, w_ukv, tabs):
    tm = TB
    scale = TM // tm
    const2 = lambda i: (0, 0)
    mod_spec = lambda k: pl.BlockSpec((None, None, 1, D), lambda i: (layer, _mod_row(i // scale), 0, k))
    tab_spec = pl.BlockSpec((tm, 128), lambda i: (
        jnp.where(i < N_LAT_TILES * scale, i % (TILES_PER_BATCH * scale), TILES_PER_BATCH * scale), 0))
    n_q = MLA_H * MLA_QK_PAD
    return pl.pallas_call(
        _mla_proj_kernel,
        grid=(N_ALL_TILES * scale,),
        in_specs=[
            pl.BlockSpec((tm, D), lambda i: (i, 0)),
            pl.BlockSpec((1, D), const2),
            mod_spec(0), mod_spec(1),
            pl.BlockSpec((D, MLA_C_PAD), const2),
            pl.BlockSpec((1, MLA_QR_PAD), const2),
            pl.BlockSpec((1, MLA_KVR), const2),
            pl.BlockSpec((MLA_QR_PAD, n_q), const2),
            pl.BlockSpec((MLA_KVR, MLA_H * (MLA_NOPE + MLA_V)), const2),
            tab_spec, tab_spec, tab_spec,
        ],
        out_specs=[
            pl.BlockSpec((1, n_q, tm), lambda i: (i, 0, 0)),
            pl.BlockSpec((tm, n_q), lambda i: (i, 0)),
            pl.BlockSpec((1, MLA_H * MLA_V, tm), lambda i: (i, 0, 0)),
        ],
        out_shape=[
            jax.ShapeDtypeStruct((T_ALL // tm, n_q, tm), BF16),
            jax.ShapeDtypeStruct((T_ALL, n_q), BF16),
            jax.ShapeDtypeStruct((T_ALL // tm, MLA_H * MLA_V, tm), BF16),
        ],
        compiler_params=_cparams(("arbitrary",)),
        name=f"mla_proj_l{layer}",
    )(xs, norm_g.reshape(1, D), mod, mod, w_d, q_g, kv_g, w_uq, w_ukv, *tabs)


def _rope_tables(rot_dim):
    a = rot_dim // 2
    ch = a // 2
    n_rows = SEQ // GRID_W
    inv = ROPE_BASE ** (-jnp.arange(0, a, 2, dtype=F32) / a)

    def axis_angles(n):
        ang = jnp.arange(n, dtype=jnp.int32).astype(F32)[:, None] * inv[None, :]
        return jnp.concatenate([ang, ang], axis=-1)

    def grid(fn):
        by_row = jnp.broadcast_to(fn(axis_angles(n_rows))[:, None, :], (n_rows, GRID_W, a))
        by_col = jnp.broadcast_to(fn(axis_angles(GRID_W))[None, :, :], (n_rows, GRID_W, a))
        return jnp.concatenate([by_row, by_col], axis=-1).reshape(SEQ, rot_dim)

    cos, sin = grid(jnp.cos), grid(jnp.sin)
    lane = jnp.arange(rot_dim)
    first = (lane % a) < ch
    sin_a = jnp.where(first[None, :], -sin, 0.0)
    sin_b = jnp.where(first[None, :], 0.0, sin)
    pad = 128 - rot_dim
    cos = jnp.pad(cos, ((0, TM), (0, pad)), constant_values=1.0)
    cos = cos.at[SEQ:, :].set(1.0)
    sin_a = jnp.pad(sin_a, ((0, TM), (0, pad)))
    sin_b = jnp.pad(sin_b, ((0, TM), (0, pad)))
    return cos, sin_a, sin_b


def kernel(x, c, ctx, c_ctx, w_mod, b_mod, norm1_g, norm2_g, w_ff1, w_ff2, a_w_in, a_b_in, a_ln_g, a_w_s, a_b_s, a_w_out, b_w_qkv, b_lam_q1, b_lam_k1, b_lam_q2, b_lam_k2, b_subln_g, b_w_o, c_w_dqkv, c_q_norm_g, c_w_uq, c_kv_norm_g, c_w_ukv, c_w_o, final_g):
    xs = (x.reshape(T_LAT, D), ctx.reshape(T_CTX, D))
    cc = jnp.concatenate([c, c_ctx[None, :], jnp.zeros((8 - BATCH - 1, D), F32)], axis=0)
    mod = _modulation(cc, w_mod, b_mod).reshape(DEPTH, 8, 1, N_MOD * D)

    tabs_b = _rope_tables(DIFF_D)
    tabs_c = _rope_tables(MLA_ROPE)
    w_ff1_b, w_ff2_b = w_ff1.astype(BF16), w_ff2.astype(BF16)

    out = None
    for l in range(DEPTH):
        kind, idx = l % 3, l // 3
        ctx_later = any(j % 3 != 0 for j in range(l + 1, DEPTH))
        last = l == DEPTH - 1
        n_tiles = N_ALL_TILES if ctx_later else N_LAT_TILES
        if kind == 0:
            b_s = jnp.broadcast_to(a_b_s[idx][:, :, None], (GROUPS, CHUNK, CHUNK))
            t = _gmlp_mix(xs, mod, l, n_tiles, norm1_g[l], a_w_in[idx].astype(BF16), a_b_in[idx],
                          a_ln_g[idx], a_w_s[idx].astype(BF16), b_s)
            xs, h2 = _out_proj(xs, t, a_w_out[idx].astype(BF16), mod, l, n_tiles, norm2_g[l])
        elif kind == 1:
            lambda_init = 0.8 - 0.6 * math.exp(-0.3 * l)
            q_t, k, v_t = _diff_proj(xs, mod, l, norm1_g[l], b_w_qkv[idx].astype(BF16), tabs_b)
            lam_tab = jnp.concatenate([b_lam_q1[idx][None], b_lam_k1[idx][None], b_lam_q2[idx][None],
                                       b_lam_k2[idx][None], jnp.zeros((4, DIFF_D), F32)], axis=0)
            o = _diff_attn(q_t, k, v_t, lam_tab, b_subln_g[idx], lambda_init, True)
            if ctx_later:
                o = (o, _diff_attn(q_t, k, v_t, lam_tab, b_subln_g[idx], lambda_init, False))
            xs, h2 = _out_proj(xs, o, b_w_o[idx].astype(BF16), mod, l, n_tiles, norm2_g[l])
        else:
            w_d = c_w_dqkv[idx]
            w_d = jnp.concatenate([
                jnp.pad(w_d[:, :MLA_QR], ((0, 0), (0, MLA_QR_PAD - MLA_QR))),
                w_d[:, MLA_QR:MLA_QR + MLA_KVR],
                jnp.pad(w_d[:, MLA_QR + MLA_KVR:], ((0, 0), (0, 128 - MLA_ROPE)))], axis=1).astype(BF16)
            w_uq = jnp.pad(c_w_uq[idx].reshape(MLA_QR, MLA_H, MLA_QK),
                           ((0, MLA_QR_PAD - MLA_QR), (0, 0), (0, MLA_QK_PAD - MLA_QK)))
            w_uq = w_uq.reshape(MLA_QR_PAD, MLA_H * MLA_QK_PAD).astype(BF16)
            w_ukv = c_w_ukv[idx].reshape(MLA_KVR, MLA_H, MLA_NOPE + MLA_V)
            w_ukv = jnp.concatenate([w_ukv[:, :, :MLA_NOPE].reshape(MLA_KVR, MLA_H * MLA_NOPE),
                                     w_ukv[:, :, MLA_NOPE:].reshape(MLA_KVR, MLA_H * MLA_V)],
                                    axis=1).astype(BF16)
            q_g = jnp.pad(c_q_norm_g[idx], (0, MLA_QR_PAD - MLA_QR)).reshape(1, MLA_QR_PAD)
            q_t, k, v_t = _mla_proj(xs, mod, l, norm1_g[l], w_d, q_g, c_kv_norm_g[idx].reshape(1, MLA_KVR),
                                    w_uq, w_ukv, tabs_c)
            o = _mla_attn(q_t, k, v_t)
            xs, h2 = _out_proj(xs, o, c_w_o[idx].astype(BF16), mod, l, N_LAT_TILES, norm2_g[l])
        if ctx_later and kind == 2:
            raise NotImplementedError("context update after an MLA layer is not needed at this depth")
        res = _ffn(xs, h2, mod, l, n_tiles, w_ff1_b, w_ff2_b, final_g, last)
        if last:
            out = res
        else:
            xs = res
    return out.reshape(BATCH, SEQ, D)
```

```python
import functools
import math

import jax
import jax.numpy as jnp
from jax import lax
from jax.experimental import pallas as pl
from jax.experimental.pallas import tpu as pltpu

D = 2048
BATCH = 2
SEQ = 8192
DEPTH = 4
GRID_W = 64
CTX = 256
N_MOD = 6
D_FF = 4 * D
EPS = 1e-6
ROPE_BASE = 10000.0

CHUNK = 128
GROUPS = 16
GDIM = 128

DIFF_H = 8
DIFF_D = 128
DIFF_V = 256

MLA_H = 16
MLA_QR = 448
MLA_QR_PAD = 512
MLA_KVR = 512
MLA_NOPE = 128
MLA_ROPE = 64
MLA_V = 128
MLA_QK = MLA_NOPE + MLA_ROPE
MLA_QK_PAD = 256
MLA_C_PAD = MLA_QR_PAD + MLA_KVR + 128

T_LAT = BATCH * SEQ
T_CTX = BATCH * CTX
T_ALL = T_LAT + T_CTX

TM = 512
TILES_PER_BATCH = SEQ // TM
N_LAT_TILES = T_LAT // TM
N_ALL_TILES = T_ALL // TM
ROPE_ID_TILE = SEQ // TM

VMEM_LIMIT = 56 * 1024 * 1024

LANES = 128
DIFF_ROPE_CH = DIFF_D // 4
MLA_ROPE_CH = MLA_ROPE // 4

BF16 = jnp.bfloat16
F32 = jnp.float32


def _cparams(sem):
    return pltpu.CompilerParams(dimension_semantics=sem, vmem_limit_bytes=VMEM_LIMIT)


def _mod_row(i):
    return jnp.minimum(i // TILES_PER_BATCH, 2)


def _mod_spec(layer, k):
    return pl.BlockSpec((None, None, 1, D), lambda i, *_: (layer, _mod_row(i), 0, k))


def _rope_tile(i):
    return jnp.where(i < N_LAT_TILES, i % TILES_PER_BATCH, ROPE_ID_TILE)


def _norm_mod(x, g, sh, sc):
    y = x * lax.rsqrt(jnp.mean(x * x, axis=-1, keepdims=True) + EPS)
    y = y * g
    return y * (1.0 + sc) + sh


def _token_specs(xs):
    if isinstance(xs, tuple):
        width = xs[0].shape[1]
        return ([pl.BlockSpec((TM, width), lambda i, *_: (jnp.minimum(i, N_LAT_TILES - 1), 0)),
                 pl.BlockSpec((TM, width), lambda i, *_: (jnp.maximum(i - N_LAT_TILES, 0), 0))], list(xs))
    return [pl.BlockSpec((TM, xs.shape[1]), lambda i, *_: (i, 0))], [xs]


def _token_tile(src_refs):
    if len(src_refs) == 2:
        return jnp.where(pl.program_id(0) < N_LAT_TILES, src_refs[0][...], src_refs[1][...])
    return src_refs[0][...]


SIDE_BANDS = 32


def _cast_band_specs(w, lead, n_bands, band):
    rows, cols = w.shape[-2:]
    rb = rows // n_bands
    in_spec = pl.BlockSpec((None, rb, cols), lambda *g: (lead, band(*g), 0))
    out_spec = pl.BlockSpec((rb, cols), lambda *g: (band(*g), 0))
    return in_spec, out_spec, jax.ShapeDtypeStruct((rows, cols), BF16)


def _rope(x, cos, sin_a, sin_b, ch):
    return (x * cos + pltpu.roll(x, LANES - ch, 1) * sin_a + pltpu.roll(x, ch, 1) * sin_b)


def _mod_kernel(c_ref, w_ref, b_ref, o_ref):
    c = c_ref[...]
    s = (c * jax.nn.sigmoid(c)).astype(BF16)
    o_ref[...] = jnp.dot(s, w_ref[...].astype(BF16), preferred_element_type=F32) + b_ref[...]


def _modulation(cc, w_mod, b_mod):
    tn = 1024
    n = N_MOD * D
    return pl.pallas_call(
        _mod_kernel,
        grid=(DEPTH, n // tn),
        in_specs=[
            pl.BlockSpec((8, D), lambda l, j: (0, 0)),
            pl.BlockSpec((None, D, tn), lambda l, j: (l, 0, j)),
            pl.BlockSpec((None, 1, tn), lambda l, j: (l, 0, j)),
        ],
        out_specs=pl.BlockSpec((None, 8, tn), lambda l, j: (l, 0, j)),
        out_shape=jax.ShapeDtypeStruct((DEPTH, 8, n), F32),
        compiler_params=_cparams(("arbitrary", "arbitrary")),
        name="modulation",
    )(cc, w_mod, b_mod.reshape(DEPTH, 1, n))


def _gelu_tanh(x):
    return 0.5 * x * (1.0 + jnp.tanh(math.sqrt(2.0 / math.pi) * (x + 0.044715 * (x * x * x))))


def _gmlp_kernel(*refs, n_src, n_cast):
    src = refs[:n_src]
    g_ref, sh_ref, sc_ref, win_ref, bin_ref, lng_ref, ws_ref, bs_ref = refs[n_src:n_src + 8]
    cast_in = refs[n_src + 8:n_src + 8 + n_cast]
    o_ref = refs[n_src + 8 + n_cast]
    cast_out = refs[n_src + 9 + n_cast:n_src + 9 + 2 * n_cast]
    vn_ref = refs[-1]
    for w_ref, wb_ref in zip(cast_in, cast_out):
        wb_ref[...] = w_ref[...].astype(BF16)
    nchunk = TM // CHUNK
    h = _norm_mod(_token_tile(src), g_ref[...], sh_ref[...], sc_ref[...]).astype(BF16)
    zv = _gelu_tanh(jnp.dot(h, win_ref[:, D:], preferred_element_type=F32) + bin_ref[:, D:])
    mu = jnp.mean(zv, axis=-1, keepdims=True)
    zc = zv - mu
    vn = zc * lax.rsqrt(jnp.mean(zc * zc, axis=-1, keepdims=True) + EPS) * lng_ref[...]
    vn_ref[...] = vn.astype(BF16)
    ublk = 512
    zus = [_gelu_tanh(jnp.dot(h, win_ref[:, ub * ublk:(ub + 1) * ublk], preferred_element_type=F32)
                      + bin_ref[:, ub * ublk:(ub + 1) * ublk]) for ub in range(D // ublk)]
    for ub, zu in enumerate(zus):
        for gg in range(ublk // GDIM):
            grp = ub * (ublk // GDIM) + gg
            cols = slice(grp * GDIM, (grp + 1) * GDIM)
            rhs = jnp.concatenate([vn_ref[c * CHUNK:(c + 1) * CHUNK, cols] for c in range(nchunk)], axis=1)
            sv = jnp.dot(ws_ref[grp], rhs, preferred_element_type=F32)
            for c in range(nchunk):
                svc = sv[:, c * CHUNK:(c + 1) * CHUNK] + bs_ref[grp]
                u = zu[c * CHUNK:(c + 1) * CHUNK, gg * GDIM:(gg + 1) * GDIM]
                o_ref[c * CHUNK:(c + 1) * CHUNK, cols] = (u * svc).astype(BF16)


def _gmlp_mix(xs, mod, layer, n_tiles, norm_g, w_in, b_in, ln_g, w_s, b_s, cast_f32=()):
    const2 = lambda i: (0, 0)
    src_specs, src_args = _token_specs(xs)
    casts = [_cast_band_specs(w, layer, SIDE_BANDS, lambda i: (i * SIDE_BANDS) // n_tiles) for w in cast_f32]
    return pl.pallas_call(
        functools.partial(_gmlp_kernel, n_src=len(src_args), n_cast=len(casts)),
        grid=(n_tiles,),
        in_specs=src_specs + [
            pl.BlockSpec((1, D), const2),
            _mod_spec(layer, 0), _mod_spec(layer, 1),
            pl.BlockSpec((D, 2 * D), const2),
            pl.BlockSpec((1, 2 * D), const2),
            pl.BlockSpec((1, D), const2),
            pl.BlockSpec((GROUPS, CHUNK, CHUNK), lambda i: (0, 0, 0)),
            pl.BlockSpec((GROUPS, CHUNK, CHUNK), lambda i: (0, 0, 0)),
        ] + [c[0] for c in casts],
        out_specs=[pl.BlockSpec((TM, D), lambda i: (i, 0))] + [c[1] for c in casts],
        out_shape=[jax.ShapeDtypeStruct((n_tiles * TM, D), BF16)] + [c[2] for c in casts],
        scratch_shapes=[pltpu.VMEM((TM, D), BF16)],
        compiler_params=_cparams(("arbitrary",)),
        name=f"gmlp_mix_l{layer}",
    )(*src_args, norm_g.reshape(1, D), mod, mod, w_in, b_in.reshape(1, 2 * D), ln_g.reshape(1, D), w_s, b_s,
      *cast_f32)


def _out_proj_kernel(*refs, n_src, n_act, n_cast):
    src = refs[:n_src]
    act = refs[n_src:n_src + n_act]
    w_ref, gate_ref, g_ref, sh_ref, sc_ref = refs[n_src + n_act:n_src + n_act + 5]
    rest = refs[n_src + n_act + 5:]
    cast_in, (o_ref, h_ref), cast_out = rest[:n_cast], rest[n_cast:n_cast + 2], rest[n_cast + 2:]
    for wf_ref, wb_ref in zip(cast_in, cast_out):
        wb_ref[...] = wf_ref[...].astype(BF16)
    y = jnp.dot(_token_tile(act), w_ref[...], preferred_element_type=F32)
    x1 = _token_tile(src) + gate_ref[...] * y
    o_ref[...] = x1
    h_ref[...] = _norm_mod(x1, g_ref[...], sh_ref[...], sc_ref[...]).astype(BF16)


def _out_proj(xs, a, w, mod, layer, n_tiles, ffn_norm_g, cast_f32=()):
    src_specs, src_args = _token_specs(xs)
    act_specs, act_args = _token_specs(a)
    tile = pl.BlockSpec((TM, D), lambda i: (i, 0))
    casts = [_cast_band_specs(wf, layer, SIDE_BANDS, lambda i: (i * SIDE_BANDS) // n_tiles) for wf in cast_f32]
    return pl.pallas_call(
        functools.partial(_out_proj_kernel, n_src=len(src_args), n_act=len(act_args), n_cast=len(casts)),
        grid=(n_tiles,),
        in_specs=src_specs + act_specs + [
            pl.BlockSpec((w.shape[0], D), lambda i: (0, 0)),
            _mod_spec(layer, 2),
            pl.BlockSpec((1, D), lambda i: (0, 0)),
            _mod_spec(layer, 3), _mod_spec(layer, 4),
        ] + [c[0] for c in casts],
        out_specs=[tile, tile] + [c[1] for c in casts],
        out_shape=([jax.ShapeDtypeStruct((T_ALL, D), F32), jax.ShapeDtypeStruct((n_tiles * TM, D), BF16)]
                   + [c[2] for c in casts]),
        input_output_aliases={0: 0} if len(src_args) == 1 else {},
        compiler_params=_cparams(("arbitrary",)),
        name=f"out_proj_l{layer}",
    )(*src_args, *act_args, w, mod, ffn_norm_g.reshape(1, D), mod, mod, *cast_f32)


CAST_BLOCKS = 128


def _ffn_kernel(*refs, n_f, final_norm, cast_next):
    x_ref, h_ref, gate_ref, w1_ref, w2_ref, fg_ref = refs[:6]
    o_ref = refs[8] if cast_next else refs[6]
    f = pl.program_id(1)

    @pl.when(f == 0)
    def _():
        o_ref[...] = jnp.zeros_like(o_ref)

    a = jnp.dot(h_ref[...], w1_ref[...], preferred_element_type=F32)
    a = jnp.square(jnp.maximum(a, 0.0)).astype(BF16)
    o_ref[...] += jnp.dot(a, w2_ref[...], preferred_element_type=F32)
    if cast_next:
        wn1_ref, wn2_ref, _, wb1_ref, wb2_ref = refs[6:]
        wb1_ref[...] = wn1_ref[...].astype(BF16)
        wb2_ref[...] = wn2_ref[...].astype(BF16)

    @pl.when(f == n_f - 1)
    def _():
        y = x_ref[...] + gate_ref[...] * o_ref[...]
        if final_norm:
            y = y * lax.rsqrt(jnp.mean(y * y, axis=-1, keepdims=True) + EPS) * fg_ref[...]
        o_ref[...] = y


def _ffn(xs, h, mod, layer, n_tiles, w1, w2, final_g, final_norm, next_f32=None):
    tf = 1024
    n_f = D_FF // tf
    rows = T_LAT if final_norm else T_ALL
    tile = pl.BlockSpec((TM, D), lambda i, f: (i, 0))
    in_specs = [
        tile, tile,
        _mod_spec(layer, 5),
        pl.BlockSpec((D, tf), lambda i, f: (0, f)),
        pl.BlockSpec((tf, D), lambda i, f: (f, 0)),
        pl.BlockSpec((1, D), lambda i, f: (0, 0)),
    ]
    args = [xs, h, mod, w1, w2, final_g.reshape(1, D)]
    out_specs = [tile]
    out_shape = [jax.ShapeDtypeStruct((rows, D), F32)]
    if next_f32 is not None:
        steps = n_tiles * n_f
        band = lambda i, f: ((i * n_f + f) * CAST_BLOCKS) // steps
        casts = [_cast_band_specs(w, layer + 1, CAST_BLOCKS, band) for w in next_f32]
        in_specs += [c[0] for c in casts]
        args += list(next_f32)
        out_specs += [c[1] for c in casts]
        out_shape += [c[2] for c in casts]
    return pl.pallas_call(
        functools.partial(_ffn_kernel, n_f=n_f, final_norm=final_norm, cast_next=next_f32 is not None),
        grid=(n_tiles, n_f),
        in_specs=in_specs,
        out_specs=out_specs,
        out_shape=out_shape,
        input_output_aliases={} if final_norm else {0: 0},
        compiler_params=_cparams(("arbitrary", "arbitrary")),
        name=f"ffn_l{layer}",
    )(*args)


TB = 256
LOG2E = 1.4426950408889634
DIFF_Q_SCALE = DIFF_D ** -0.5 * LOG2E
MLA_Q_SCALE = MLA_QK ** -0.5 * LOG2E
DIFF_BAND = DIFF_H * DIFF_V
PROJ_SUB = 512


def _diff_proj_kernel(x_ref, g_ref, sh_ref, sc_ref, w_ref, cos_ref, sa_ref, sb_ref,
                      qt_ref, k_ref, vt_ref, h_ref):
    n = pl.program_id(1)

    @pl.when(n == 0)
    def _():
        h_ref[...] = _norm_mod(x_ref[...], g_ref[...], sh_ref[...], sc_ref[...]).astype(BF16)

    def sub_dot(s):
        return jnp.dot(h_ref[...], w_ref[:, s * PROJ_SUB:(s + 1) * PROJ_SUB], preferred_element_type=F32)

    def roped(y):
        cos, sa, sb = cos_ref[...], sa_ref[...], sb_ref[...]
        return jnp.concatenate([_rope(y[:, c * LANES:(c + 1) * LANES], cos, sa, sb, DIFF_ROPE_CH)
                                for c in range(PROJ_SUB // LANES)], axis=1)

    def store_t(ref, s, y):
        for t in range(TM // TB):
            ref[t, s * PROJ_SUB:(s + 1) * PROJ_SUB, :] = y[t * TB:(t + 1) * TB, :].T.astype(BF16)

    @pl.when(n == 0)
    def _():
        for s in range(DIFF_BAND // PROJ_SUB):
            store_t(qt_ref, s, roped(sub_dot(s) * DIFF_Q_SCALE))

    @pl.when(n == 1)
    def _():
        for s in range(DIFF_BAND // PROJ_SUB):
            k_ref[:, s * PROJ_SUB:(s + 1) * PROJ_SUB] = roped(sub_dot(s)).astype(BF16)

    @pl.when(n == 2)
    def _():
        for s in range(DIFF_BAND // PROJ_SUB):
            store_t(vt_ref, s, sub_dot(s))


def _diff_proj(xs, mod, layer, norm_g, w_qkv, tabs):
    const2 = lambda i, n: (0, 0)
    tab_spec = pl.BlockSpec((TM, LANES), lambda i, n: (_rope_tile(i), 0))
    t_spec = pl.BlockSpec((TM // TB, DIFF_BAND, TB), lambda i, n: (i, 0, 0))
    t_shape = jax.ShapeDtypeStruct((T_ALL // TB, DIFF_BAND, TB), BF16)
    return pl.pallas_call(
        _diff_proj_kernel,
        grid=(N_ALL_TILES, 3),
        in_specs=[
            pl.BlockSpec((TM, D), lambda i, n: (i, 0)),
            pl.BlockSpec((1, D), const2),
            _mod_spec(layer, 0), _mod_spec(layer, 1),
            pl.BlockSpec((D, DIFF_BAND), lambda i, n: (0, n)),
            tab_spec, tab_spec, tab_spec,
        ],
        out_specs=[t_spec, pl.BlockSpec((TM, DIFF_BAND), lambda i, n: (i, 0)), t_spec],
        out_shape=[t_shape, jax.ShapeDtypeStruct((T_ALL, DIFF_BAND), BF16), t_shape],
        scratch_shapes=[pltpu.VMEM((TM, D), BF16)],
        compiler_params=_cparams(("arbitrary", "arbitrary")),
        name=f"diff_proj_l{layer}",
    )(xs, norm_g.reshape(1, D), mod, mod, w_qkv, *tabs)


TQ = 1024
TK = 1024
SUM_ROWS = 16


def _scores(k, q_t):
    return jnp.dot(k, q_t, preferred_element_type=F32)


def _attn_update(s, m_blk, v_t_parts, state, first):
    m_ref, l_ref, acc_ref = state
    dv = acc_ref.shape[0]
    m_new = m_blk if first else jnp.maximum(m_ref[...], m_blk)
    pb = jnp.exp2(s - m_new).astype(BF16)
    ones = jnp.ones((SUM_ROWS, TB), BF16)
    pv = None
    for t, v_t in enumerate(v_t_parts):
        d = jnp.dot(jnp.concatenate([v_t, ones], axis=0), pb[t * TB:(t + 1) * TB, :],
                    preferred_element_type=F32)
        pv = d if pv is None else pv + d
    l_blk = pv[dv:dv + 1, :]
    if first:
        l_ref[...] = l_blk
        acc_ref[...] = pv[:dv, :]
    else:
        alpha = jnp.exp2(m_ref[...] - m_new)
        l_ref[...] = alpha * l_ref[...] + l_blk
        acc_ref[...] = alpha * acc_ref[...] + pv[:dv, :]
    m_ref[...] = m_new


def _attn_pipeline(score_fn, v_fn, head_scores, head_v, states, slot_a, slot_b, n):
    def stash(j, slot):
        for s, (s_ref, mb_ref) in zip(score_fn(j), slot):
            s_ref[...] = s
            mb_ref[...] = jnp.max(s, axis=0, keepdims=True)

    def update(j, slot):
        for (s_ref, mb_ref), v_parts, state in zip(slot, v_fn(j), states):
            _attn_update(s_ref[...], mb_ref[...], v_parts, state, False)

    stash(0, slot_a)
    for s, v_parts, state in zip(head_scores, head_v, states):
        _attn_update(s, jnp.max(s, axis=0, keepdims=True), v_parts, state, True)

    def pair(i, carry):
        j = 2 * i
        stash(j + 1, slot_b)
        update(j, slot_a)
        stash(j + 2, slot_a)
        update(j + 1, slot_b)
        return carry

    lax.fori_loop(0, n // 2 - 1, pair, 0)
    stash(n - 1, slot_b)
    update(n - 2, slot_a)
    update(n - 1, slot_b)


def _load_q_t(qt_ref, rows):
    return jnp.concatenate([qt_ref[t, rows, :] for t in range(qt_ref.shape[0])], axis=1)


def _lat_chunk_fns(k_ref, vt_ref, v_rows):
    k_fn = lambda j: k_ref[pl.ds(pl.multiple_of(j * TK, TK), TK), :]
    v_fn = lambda j: [[vt_ref[(TK // TB) * j + t, rows, :] for t in range(TK // TB)] for rows in v_rows]
    return k_fn, v_fn


def _slot_scratch(n_chain, tq):
    return [pltpu.VMEM((TK, tq), F32), pltpu.VMEM((1, tq), F32)] * (2 * n_chain)


def _split_slots(refs, n_chain):
    pairs = [(refs[2 * i], refs[2 * i + 1]) for i in range(2 * n_chain)]
    return pairs[:n_chain], pairs[n_chain:]


def _diff_attn_kernel(*refs, has_lat, lambda_init):
    if has_lat:
        qt_ref, kc_ref, vtc_ref, k_ref, vt_ref, lam_ref, sg_ref, o_ref = refs[:8]
        scratch = refs[8:]
    else:
        qt_ref, kc_ref, vtc_ref, lam_ref, sg_ref, o_ref = refs[:6]
        scratch = refs[6:]
    states = (scratch[0:3], scratch[3:6])
    halves = (slice(0, DIFF_D), slice(DIFF_D, 2 * DIFF_D))

    def scores_of(k):
        return [_scores(k[:, halves[idx]], _load_q_t(qt_ref, halves[idx])) for idx in range(2)]

    ctx_scores = scores_of(kc_ref[...])
    v_rows = [slice(0, DIFF_V)] * 2
    ctx_v = [[vtc_ref[0]]] * 2
    if has_lat:
        k_fn, v_fn = _lat_chunk_fns(k_ref, vt_ref, v_rows)
        slot_a, slot_b = _split_slots(scratch[6:], 2)
        _attn_pipeline(lambda j: scores_of(k_fn(j)), v_fn, ctx_scores, ctx_v, states,
                       slot_a, slot_b, SEQ // TK)
    else:
        for s, v_parts, state in zip(ctx_scores, ctx_v, states):
            _attn_update(s, jnp.max(s, axis=0, keepdims=True), v_parts, state, True)

    lam_v = lam_ref[...]
    lam = (jnp.exp(jnp.sum(lam_v[0:1] * lam_v[1:2], axis=-1, keepdims=True))
           - jnp.exp(jnp.sum(lam_v[2:3] * lam_v[3:4], axis=-1, keepdims=True)) + lambda_init)
    (_, l0, a0), (_, l1, a1) = states
    o_t = a0[...] * (1.0 / l0[...]) - lam * (a1[...] * (1.0 / l1[...]))
    o = o_t.T
    o = o * lax.rsqrt(jnp.mean(o * o, axis=-1, keepdims=True) + EPS) * sg_ref[...]
    o_ref[...] = (o * (1.0 - lambda_init)).astype(BF16)


def _diff_attn(q_t, k, v_t, lam_tab, subln_g, lambda_init, has_lat):
    ctx_blk0 = T_LAT // CTX
    if has_lat:
        tq = TQ
        grid = (BATCH, DIFF_H, SEQ // tq)
        q_blk = lambda b, h, i: b * (SEQ // tq) + i
        o_blk = q_blk
    else:
        tq = CTX
        grid = (BATCH, DIFF_H, 1)
        q_blk = lambda b, h, i: ctx_blk0 + b
        o_blk = lambda b, h, i: b
    in_specs = [
        pl.BlockSpec((tq // TB, DIFF_V, TB), lambda b, h, i: (q_blk(b, h, i), h, 0)),
        pl.BlockSpec((CTX, DIFF_V), lambda b, h, i: (ctx_blk0 + b, h)),
        pl.BlockSpec((1, DIFF_V, TB), lambda b, h, i: (ctx_blk0 + b, h, 0)),
    ]
    args = [q_t, k, v_t]
    if has_lat:
        in_specs += [
            pl.BlockSpec((SEQ, DIFF_V), lambda b, h, i: (b, h)),
            pl.BlockSpec((SEQ // TB, DIFF_V, TB), lambda b, h, i: (b, h, 0)),
        ]
        args += [k, v_t]
    in_specs += [
        pl.BlockSpec((8, DIFF_D), lambda b, h, i: (0, 0)),
        pl.BlockSpec((1, DIFF_V), lambda b, h, i: (0, 0)),
    ]
    args += [lam_tab, subln_g.reshape(1, DIFF_V)]
    return pl.pallas_call(
        functools.partial(_diff_attn_kernel, has_lat=has_lat, lambda_init=lambda_init),
        grid=grid,
        in_specs=in_specs,
        out_specs=pl.BlockSpec((tq, DIFF_V), lambda b, h, i: (o_blk(b, h, i), h)),
        out_shape=jax.ShapeDtypeStruct((T_LAT if has_lat else T_CTX, DIFF_H * DIFF_V), BF16),
        scratch_shapes=([pltpu.VMEM((1, tq), F32), pltpu.VMEM((1, tq), F32), pltpu.VMEM((DIFF_V, tq), F32)] * 2
                        + (_slot_scratch(2, tq) if has_lat else [])),
        compiler_params=_cparams(("arbitrary",) * 3),
        name="diff_attn_lat" if has_lat else "diff_attn_ctx",
    )(*args)


MLA_HPS = 2


def _mla_attn_kernel(qt_ref, kc_ref, vtc_ref, k_ref, vt_ref, o_ref, *scratch):
    states = [scratch[3 * i:3 * i + 3] for i in range(MLA_HPS)]
    qk = [slice(i * MLA_QK_PAD, (i + 1) * MLA_QK_PAD) for i in range(MLA_HPS)]
    v_rows = [slice(i * MLA_V, (i + 1) * MLA_V) for i in range(MLA_HPS)]

    def scores_of(k):
        return [_scores(k[:, qk[i]], _load_q_t(qt_ref, qk[i])) for i in range(MLA_HPS)]

    k_fn, v_fn = _lat_chunk_fns(k_ref, vt_ref, v_rows)
    slot_a, slot_b = _split_slots(scratch[3 * MLA_HPS:], MLA_HPS)
    _attn_pipeline(lambda j: scores_of(k_fn(j)), v_fn, scores_of(kc_ref[...]),
                   [[vtc_ref[0, rows, :]] for rows in v_rows], states, slot_a, slot_b, SEQ // TK)
    for i, (_, l_ref, acc_ref) in enumerate(states):
        o_ref[:, v_rows[i]] = (acc_ref[...] * (1.0 / l_ref[...])).T.astype(BF16)


def _mla_attn(q_t, k, v_t):
    ctx_blk0 = T_LAT // CTX
    qk_w, v_w = MLA_HPS * MLA_QK_PAD, MLA_HPS * MLA_V
    return pl.pallas_call(
        _mla_attn_kernel,
        grid=(BATCH, MLA_H // MLA_HPS, SEQ // TQ),
        in_specs=[
            pl.BlockSpec((TQ // TB, qk_w, TB), lambda b, h, i: (b * (SEQ // TQ) + i, h, 0)),
            pl.BlockSpec((CTX, qk_w), lambda b, h, i: (ctx_blk0 + b, h)),
            pl.BlockSpec((1, v_w, TB), lambda b, h, i: (ctx_blk0 + b, h, 0)),
            pl.BlockSpec((SEQ, qk_w), lambda b, h, i: (b, h)),
            pl.BlockSpec((SEQ // TB, v_w, TB), lambda b, h, i: (b, h, 0)),
        ],
        out_specs=pl.BlockSpec((TQ, v_w), lambda b, h, i: (b * (SEQ // TQ) + i, h)),
        out_shape=jax.ShapeDtypeStruct((T_LAT, MLA_H * MLA_V), BF16),
        scratch_shapes=([pltpu.VMEM((1, TQ), F32), pltpu.VMEM((1, TQ), F32), pltpu.VMEM((MLA_V, TQ), F32)]
                        * MLA_HPS + _slot_scratch(MLA_HPS, TQ)),
        compiler_params=_cparams(("arbitrary",) * 3),
        name="mla_attn",
    )(q_t, k, v_t, k, v_t)


def _mla_proj_kernel(x_ref, g_ref, sh_ref, sc_ref, wd_ref, qg_ref, kvg_ref, wuq_ref, wukv_ref,
                     cos_ref, sa_ref, sb_ref, qt_ref, k_ref, vt_ref):
    h = _norm_mod(x_ref[...], g_ref[...], sh_ref[...], sc_ref[...]).astype(BF16)
    c = jnp.dot(h, wd_ref[...], preferred_element_type=F32)
    cq = c[:, :MLA_QR_PAD]
    ckv = c[:, MLA_QR_PAD:MLA_QR_PAD + MLA_KVR]
    kr = c[:, MLA_QR_PAD + MLA_KVR:]
    cqn = cq * lax.rsqrt(jnp.sum(cq * cq, axis=-1, keepdims=True) * (1.0 / MLA_QR) + EPS) * qg_ref[...]
    ckvn = ckv * lax.rsqrt(jnp.mean(ckv * ckv, axis=-1, keepdims=True) + EPS) * kvg_ref[...]
    cos, sa, sb = cos_ref[...], sa_ref[...], sb_ref[...]
    kr_rot = _rope(kr, cos, sa, sb, MLA_ROPE_CH).astype(BF16)
    q = jnp.dot(cqn.astype(BF16), wuq_ref[...], preferred_element_type=F32) * MLA_Q_SCALE
    kv = jnp.dot(ckvn.astype(BF16), wukv_ref[...], preferred_element_type=F32)
    for hd in range(MLA_H):
        base = hd * MLA_QK_PAD
        nope, rope = slice(base, base + MLA_NOPE), slice(base + MLA_NOPE, base + MLA_QK_PAD)
        qt_ref[0, nope, :] = q[:, nope].T.astype(BF16)
        qt_ref[0, rope, :] = _rope(q[:, rope], cos, sa, sb, MLA_ROPE_CH).T.astype(BF16)
        k_ref[:, nope] = kv[:, hd * MLA_NOPE:(hd + 1) * MLA_NOPE].astype(BF16)
        k_ref[:, rope] = kr_rot
        vcols = slice(MLA_H * MLA_NOPE + hd * MLA_V, MLA_H * MLA_NOPE + (hd + 1) * MLA_V)
        vt_ref[0, hd * MLA_V:(hd + 1) * MLA_V, :] = kv[:, vcols].T.astype(BF16)


def _mla_proj(xs, mod, layer, norm_g, w_d, q_g, kv_g, w_uq, w_ukv, tabs):
    tm = TB
    scale = TM // tm
    const2 = lambda i: (0, 0)
    mod_spec = lambda k: pl.BlockSpec((None, None, 1, D), lambda i: (layer, _mod_row(i // scale), 0, k))
    tab_spec = pl.BlockSpec((tm, LANES), lambda i: (
        jnp.where(i < N_LAT_TILES * scale, i % (TILES_PER_BATCH * scale), TILES_PER_BATCH * scale), 0))
    n_q = MLA_H * MLA_QK_PAD
    return pl.pallas_call(
        _mla_proj_kernel,
        grid=(N_ALL_TILES * scale,),
        in_specs=[
            pl.BlockSpec((tm, D), lambda i: (i, 0)),
            pl.BlockSpec((1, D), const2),
            mod_spec(0), mod_spec(1),
            pl.BlockSpec((D, MLA_C_PAD), const2),
            pl.BlockSpec((1, MLA_QR_PAD), const2),
            pl.BlockSpec((1, MLA_KVR), const2),
            pl.BlockSpec((MLA_QR_PAD, n_q), const2),
            pl.BlockSpec((MLA_KVR, MLA_H * (MLA_NOPE + MLA_V)), const2),
            tab_spec, tab_spec, tab_spec,
        ],
        out_specs=[
            pl.BlockSpec((1, n_q, tm), lambda i: (i, 0, 0)),
            pl.BlockSpec((tm, n_q), lambda i: (i, 0)),
            pl.BlockSpec((1, MLA_H * MLA_V, tm), lambda i: (i, 0, 0)),
        ],
        out_shape=[
            jax.ShapeDtypeStruct((T_ALL // tm, n_q, tm), BF16),
            jax.ShapeDtypeStruct((T_ALL, n_q), BF16),
            jax.ShapeDtypeStruct((T_ALL // tm, MLA_H * MLA_V, tm), BF16),
        ],
        compiler_params=_cparams(("arbitrary",)),
        name=f"mla_proj_l{layer}",
    )(xs, norm_g.reshape(1, D), mod, mod, w_d, q_g, kv_g, w_uq, w_ukv, *tabs)


def _rope_tables(rot_dim):
    a = rot_dim // 2
    ch = a // 2
    n_rows = SEQ // GRID_W
    inv = ROPE_BASE ** (-jnp.arange(0, a, 2, dtype=F32) / a)

    def axis_angles(n):
        ang = jnp.arange(n, dtype=jnp.int32).astype(F32)[:, None] * inv[None, :]
        return jnp.concatenate([ang, ang], axis=-1)

    def grid(fn):
        by_row = jnp.broadcast_to(fn(axis_angles(n_rows))[:, None, :], (n_rows, GRID_W, a))
        by_col = jnp.broadcast_to(fn(axis_angles(GRID_W))[None, :, :], (n_rows, GRID_W, a))
        return jnp.concatenate([by_row, by_col], axis=-1).reshape(SEQ, rot_dim)

    cos, sin = grid(jnp.cos), grid(jnp.sin)
    lane = jnp.arange(rot_dim)
    first = (lane % a) < ch
    sin_a = jnp.where(first[None, :], -sin, 0.0)
    sin_b = jnp.where(first[None, :], 0.0, sin)
    pad = LANES - rot_dim
    cos = jnp.pad(cos, ((0, TM), (0, pad)), constant_values=1.0)
    cos = cos.at[SEQ:, :].set(1.0)
    sin_a = jnp.pad(sin_a, ((0, TM), (0, pad)))
    sin_b = jnp.pad(sin_b, ((0, TM), (0, pad)))
    return cos, sin_a, sin_b


def kernel(x, c, ctx, c_ctx, w_mod, b_mod, norm1_g, norm2_g, w_ff1, w_ff2, a_w_in, a_b_in, a_ln_g, a_w_s, a_b_s, a_w_out, b_w_qkv, b_lam_q1, b_lam_k1, b_lam_q2, b_lam_k2, b_subln_g, b_w_o, c_w_dqkv, c_q_norm_g, c_w_uq, c_kv_norm_g, c_w_ukv, c_w_o, final_g):
    xs = (x.reshape(T_LAT, D), ctx.reshape(T_CTX, D))
    cc = jnp.concatenate([c, c_ctx[None, :], jnp.zeros((8 - BATCH - 1, D), F32)], axis=0)
    mod = _modulation(cc, w_mod, b_mod).reshape(DEPTH, 8, 1, N_MOD * D)

    tabs_b = _rope_tables(DIFF_D)
    tabs_c = _rope_tables(MLA_ROPE)
    w1_b = w2_b = None

    out = None
    for l in range(DEPTH):
        kind, idx = l % 3, l // 3
        ctx_later = any(j % 3 != 0 for j in range(l + 1, DEPTH))
        last = l == DEPTH - 1
        n_tiles = N_ALL_TILES if ctx_later else N_LAT_TILES
        if kind == 0:
            b_s = jnp.broadcast_to(a_b_s[idx][:, :, None], (GROUPS, CHUNK, CHUNK))
            t, *cast = _gmlp_mix(xs, mod, l, n_tiles, norm1_g[l], a_w_in[idx].astype(BF16), a_b_in[idx],
                                 a_ln_g[idx], a_w_s[idx].astype(BF16), b_s, (w_ff1,) if l == 0 else ())
            xs, h2, *cast2 = _out_proj(xs, t, a_w_out[idx].astype(BF16), mod, l, n_tiles, norm2_g[l],
                                       (w_ff2,) if l == 0 else ())
            if l == 0:
                (w1_b,), (w2_b,) = cast, cast2
        elif kind == 1:
            lambda_init = 0.8 - 0.6 * math.exp(-0.3 * l)
            q_t, k, v_t = _diff_proj(xs, mod, l, norm1_g[l], b_w_qkv[idx].astype(BF16), tabs_b)
            lam_tab = jnp.concatenate([b_lam_q1[idx][None], b_lam_k1[idx][None], b_lam_q2[idx][None],
                                       b_lam_k2[idx][None], jnp.zeros((4, DIFF_D), F32)], axis=0)
            o = _diff_attn(q_t, k, v_t, lam_tab, b_subln_g[idx], lambda_init, True)
            if ctx_later:
                o = (o, _diff_attn(q_t, k, v_t, lam_tab, b_subln_g[idx], lambda_init, False))
            xs, h2 = _out_proj(xs, o, b_w_o[idx].astype(BF16), mod, l, n_tiles, norm2_g[l])
        else:
            w_d = c_w_dqkv[idx]
            w_d = jnp.concatenate([
                jnp.pad(w_d[:, :MLA_QR], ((0, 0), (0, MLA_QR_PAD - MLA_QR))),
                w_d[:, MLA_QR:MLA_QR + MLA_KVR],
                jnp.pad(w_d[:, MLA_QR + MLA_KVR:], ((0, 0), (0, LANES - MLA_ROPE)))], axis=1).astype(BF16)
            w_uq = jnp.pad(c_w_uq[idx].reshape(MLA_QR, MLA_H, MLA_QK),
                           ((0, MLA_QR_PAD - MLA_QR), (0, 0), (0, MLA_QK_PAD - MLA_QK)))
            w_uq = w_uq.reshape(MLA_QR_PAD, MLA_H * MLA_QK_PAD).astype(BF16)
            w_ukv = c_w_ukv[idx].reshape(MLA_KVR, MLA_H, MLA_NOPE + MLA_V)
            w_ukv = jnp.concatenate([w_ukv[:, :, :MLA_NOPE].reshape(MLA_KVR, MLA_H * MLA_NOPE),
                                     w_ukv[:, :, MLA_NOPE:].reshape(MLA_KVR, MLA_H * MLA_V)],
                                    axis=1).astype(BF16)
            q_g = jnp.pad(c_q_norm_g[idx], (0, MLA_QR_PAD - MLA_QR)).reshape(1, MLA_QR_PAD)
            q_t, k, v_t = _mla_proj(xs, mod, l, norm1_g[l], w_d, q_g, c_kv_norm_g[idx].reshape(1, MLA_KVR),
                                    w_uq, w_ukv, tabs_c)
            o = _mla_attn(q_t, k, v_t)
            xs, h2 = _out_proj(xs, o, c_w_o[idx].astype(BF16), mod, l, N_LAT_TILES, norm2_g[l])
        if ctx_later and kind == 2:
            raise NotImplementedError("context update after an MLA layer is not needed at this depth")
        if w1_b is None:
            w1_b, w2_b = w_ff1[l].astype(BF16), w_ff2[l].astype(BF16)
        res = _ffn(xs, h2, mod, l, n_tiles, w1_b, w2_b, final_g, last, None if last else (w_ff1, w_ff2))
        if last:
            out, = res
        else:
            xs, w1_b, w2_b = res
    return out.reshape(BATCH, SEQ, D)
```

```python
import functools
import math

import jax
import jax.numpy as jnp
from jax import lax
from jax.experimental import pallas as pl
from jax.experimental.pallas import tpu as pltpu

D = 2048
BATCH = 2
SEQ = 8192
DEPTH = 4
GRID_W = 64
CTX = 256
N_MOD = 6
D_FF = 4 * D
EPS = 1e-6
ROPE_BASE = 10000.0

CHUNK = 128
GROUPS = 16
GDIM = 128

DIFF_H = 8
DIFF_D = 128
DIFF_V = 256

MLA_H = 16
MLA_QR = 448
MLA_QR_PAD = 512
MLA_KVR = 512
MLA_NOPE = 128
MLA_ROPE = 64
MLA_V = 128
MLA_QK = MLA_NOPE + MLA_ROPE
MLA_QK_PAD = 256
MLA_C_PAD = MLA_QR_PAD + MLA_KVR + 128

T_LAT = BATCH * SEQ
T_CTX = BATCH * CTX
T_ALL = T_LAT + T_CTX

TM = 512
TILES_PER_BATCH = SEQ // TM
N_LAT_TILES = T_LAT // TM
N_ALL_TILES = T_ALL // TM
ROPE_ID_TILE = SEQ // TM

VMEM_LIMIT = 56 * 1024 * 1024

LANES = 128
DIFF_ROPE_CH = DIFF_D // 4
MLA_ROPE_CH = MLA_ROPE // 4

BF16 = jnp.bfloat16
F32 = jnp.float32


def _cparams(sem):
    return pltpu.CompilerParams(dimension_semantics=sem, vmem_limit_bytes=VMEM_LIMIT)


def _mod_row(i):
    return jnp.minimum(i // TILES_PER_BATCH, 2)


def _mod_spec(layer, k):
    return pl.BlockSpec((None, None, 1, D), lambda i, *_: (layer, _mod_row(i), 0, k))


def _rope_tile(i):
    return jnp.where(i < N_LAT_TILES, i % TILES_PER_BATCH, ROPE_ID_TILE)


def _norm_mod(x, g, sh, sc):
    y = x * lax.rsqrt(jnp.mean(x * x, axis=-1, keepdims=True) + EPS)
    y = y * g
    return y * (1.0 + sc) + sh


def _token_specs(xs):
    if isinstance(xs, tuple):
        width = xs[0].shape[1]
        return ([pl.BlockSpec((TM, width), lambda i, *_: (jnp.minimum(i, N_LAT_TILES - 1), 0)),
                 pl.BlockSpec((TM, width), lambda i, *_: (jnp.maximum(i - N_LAT_TILES, 0), 0))], list(xs))
    return [pl.BlockSpec((TM, xs.shape[1]), lambda i, *_: (i, 0))], [xs]


def _token_tile(src_refs):
    if len(src_refs) == 2:
        return jnp.where(pl.program_id(0) < N_LAT_TILES, src_refs[0][...], src_refs[1][...])
    return src_refs[0][...]


def _rope(x, cos, sin_a, sin_b, ch):
    return (x * cos + pltpu.roll(x, LANES - ch, 1) * sin_a + pltpu.roll(x, ch, 1) * sin_b)


def _mod_kernel(c_ref, w_ref, b_ref, o_ref):
    c = c_ref[...]
    s = (c * jax.nn.sigmoid(c)).astype(BF16)
    o_ref[...] = jnp.dot(s, w_ref[...].astype(BF16), preferred_element_type=F32) + b_ref[...]


def _modulation(cc, w_mod, b_mod):
    tn = 1024
    n = N_MOD * D
    return pl.pallas_call(
        _mod_kernel,
        grid=(DEPTH, n // tn),
        in_specs=[
            pl.BlockSpec((8, D), lambda l, j: (0, 0)),
            pl.BlockSpec((None, D, tn), lambda l, j: (l, 0, j)),
            pl.BlockSpec((None, 1, tn), lambda l, j: (l, 0, j)),
        ],
        out_specs=pl.BlockSpec((None, 8, tn), lambda l, j: (l, 0, j)),
        out_shape=jax.ShapeDtypeStruct((DEPTH, 8, n), F32),
        compiler_params=_cparams(("arbitrary", "arbitrary")),
        name="modulation",
    )(cc, w_mod, b_mod.reshape(DEPTH, 1, n))


def _gelu_tanh(x):
    return 0.5 * x * (1.0 + jnp.tanh(math.sqrt(2.0 / math.pi) * (x + 0.044715 * (x * x * x))))


def _gmlp_kernel(*refs, n_src):
    src = refs[:n_src]
    g_ref, sh_ref, sc_ref, win_ref, bin_ref, lng_ref, ws_ref, bs_ref, o_ref, vn_ref = refs[n_src:]
    nchunk = TM // CHUNK
    h = _norm_mod(_token_tile(src), g_ref[...], sh_ref[...], sc_ref[...]).astype(BF16)
    zv = _gelu_tanh(jnp.dot(h, win_ref[:, D:], preferred_element_type=F32) + bin_ref[:, D:])
    mu = jnp.mean(zv, axis=-1, keepdims=True)
    zc = zv - mu
    vn = zc * lax.rsqrt(jnp.mean(zc * zc, axis=-1, keepdims=True) + EPS) * lng_ref[...]
    vn_ref[...] = vn.astype(BF16)
    ublk = 512
    zus = [_gelu_tanh(jnp.dot(h, win_ref[:, ub * ublk:(ub + 1) * ublk], preferred_element_type=F32)
                      + bin_ref[:, ub * ublk:(ub + 1) * ublk]) for ub in range(D // ublk)]
    for ub, zu in enumerate(zus):
        for gg in range(ublk // GDIM):
            grp = ub * (ublk // GDIM) + gg
            cols = slice(grp * GDIM, (grp + 1) * GDIM)
            rhs = jnp.concatenate([vn_ref[c * CHUNK:(c + 1) * CHUNK, cols] for c in range(nchunk)], axis=1)
            sv = jnp.dot(ws_ref[grp], rhs, preferred_element_type=F32)
            for c in range(nchunk):
                svc = sv[:, c * CHUNK:(c + 1) * CHUNK] + bs_ref[grp]
                u = zu[c * CHUNK:(c + 1) * CHUNK, gg * GDIM:(gg + 1) * GDIM]
                o_ref[c * CHUNK:(c + 1) * CHUNK, cols] = (u * svc).astype(BF16)


def _gmlp_mix(xs, mod, layer, n_tiles, norm_g, w_in, b_in, ln_g, w_s, b_s):
    const2 = lambda i: (0, 0)
    src_specs, src_args = _token_specs(xs)
    return pl.pallas_call(
        functools.partial(_gmlp_kernel, n_src=len(src_args)),
        grid=(n_tiles,),
        in_specs=src_specs + [
            pl.BlockSpec((1, D), const2),
            _mod_spec(layer, 0), _mod_spec(layer, 1),
            pl.BlockSpec((D, 2 * D), const2),
            pl.BlockSpec((1, 2 * D), const2),
            pl.BlockSpec((1, D), const2),
            pl.BlockSpec((GROUPS, CHUNK, CHUNK), lambda i: (0, 0, 0)),
            pl.BlockSpec((GROUPS, CHUNK, CHUNK), lambda i: (0, 0, 0)),
        ],
        out_specs=pl.BlockSpec((TM, D), lambda i: (i, 0)),
        out_shape=jax.ShapeDtypeStruct((n_tiles * TM, D), BF16),
        scratch_shapes=[pltpu.VMEM((TM, D), BF16)],
        compiler_params=_cparams(("arbitrary",)),
        name=f"gmlp_mix_l{layer}",
    )(*src_args, norm_g.reshape(1, D), mod, mod, w_in, b_in.reshape(1, 2 * D), ln_g.reshape(1, D), w_s, b_s)


def _out_proj_kernel(*refs, n_src, n_act):
    src = refs[:n_src]
    act = refs[n_src:n_src + n_act]
    w_ref, gate_ref, g_ref, sh_ref, sc_ref, o_ref, h_ref = refs[n_src + n_act:]
    y = jnp.dot(_token_tile(act), w_ref[...], preferred_element_type=F32)
    x1 = _token_tile(src) + gate_ref[...] * y
    o_ref[...] = x1
    h_ref[...] = _norm_mod(x1, g_ref[...], sh_ref[...], sc_ref[...]).astype(BF16)


def _out_proj(xs, a, w, mod, layer, n_tiles, ffn_norm_g):
    src_specs, src_args = _token_specs(xs)
    act_specs, act_args = _token_specs(a)
    tile = pl.BlockSpec((TM, D), lambda i: (i, 0))
    return pl.pallas_call(
        functools.partial(_out_proj_kernel, n_src=len(src_args), n_act=len(act_args)),
        grid=(n_tiles,),
        in_specs=src_specs + act_specs + [
            pl.BlockSpec((w.shape[0], D), lambda i: (0, 0)),
            _mod_spec(layer, 2),
            pl.BlockSpec((1, D), lambda i: (0, 0)),
            _mod_spec(layer, 3), _mod_spec(layer, 4),
        ],
        out_specs=[tile, tile],
        out_shape=[jax.ShapeDtypeStruct((T_ALL, D), F32), jax.ShapeDtypeStruct((n_tiles * TM, D), BF16)],
        input_output_aliases={0: 0} if len(src_args) == 1 else {},
        compiler_params=_cparams(("arbitrary",)),
        name=f"out_proj_l{layer}",
    )(*src_args, *act_args, w, mod, ffn_norm_g.reshape(1, D), mod, mod)


CAST_BLOCKS = 128


def _ffn_kernel(*refs, n_f, final_norm, cast_next):
    x_ref, h_ref, gate_ref, w1_ref, w2_ref, fg_ref = refs[:6]
    o_ref = refs[8] if cast_next else refs[6]
    f = pl.program_id(1)

    @pl.when(f == 0)
    def _():
        o_ref[...] = jnp.zeros_like(o_ref)

    a = jnp.dot(h_ref[...], w1_ref[...], preferred_element_type=F32)
    a = jnp.square(jnp.maximum(a, 0.0)).astype(BF16)
    o_ref[...] += jnp.dot(a, w2_ref[...], preferred_element_type=F32)
    if cast_next:
        wn1_ref, wn2_ref, _, wb1_ref, wb2_ref = refs[6:]
        wb1_ref[...] = wn1_ref[...].astype(BF16)
        wb2_ref[...] = wn2_ref[...].astype(BF16)

    @pl.when(f == n_f - 1)
    def _():
        y = x_ref[...] + gate_ref[...] * o_ref[...]
        if final_norm:
            y = y * lax.rsqrt(jnp.mean(y * y, axis=-1, keepdims=True) + EPS) * fg_ref[...]
        o_ref[...] = y


def _ffn(xs, h, mod, layer, n_tiles, w1, w2, final_g, final_norm, next_f32=None):
    tf = 1024
    n_f = D_FF // tf
    rows = T_LAT if final_norm else T_ALL
    tile = pl.BlockSpec((TM, D), lambda i, f: (i, 0))
    in_specs = [
        tile, tile,
        _mod_spec(layer, 5),
        pl.BlockSpec((D, tf), lambda i, f: (0, f)),
        pl.BlockSpec((tf, D), lambda i, f: (f, 0)),
        pl.BlockSpec((1, D), lambda i, f: (0, 0)),
    ]
    args = [xs, h, mod, w1, w2, final_g.reshape(1, D)]
    out_specs = [tile]
    out_shape = [jax.ShapeDtypeStruct((rows, D), F32)]
    if next_f32 is not None:
        steps = n_tiles * n_f
        band = lambda i, f: ((i * n_f + f) * CAST_BLOCKS) // steps
        r1, r2 = D // CAST_BLOCKS, D_FF // CAST_BLOCKS
        in_specs += [pl.BlockSpec((None, r1, D_FF), lambda i, f: (layer + 1, band(i, f), 0)),
                     pl.BlockSpec((None, r2, D), lambda i, f: (layer + 1, band(i, f), 0))]
        args += list(next_f32)
        out_specs += [pl.BlockSpec((r1, D_FF), lambda i, f: (band(i, f), 0)),
                      pl.BlockSpec((r2, D), lambda i, f: (band(i, f), 0))]
        out_shape += [jax.ShapeDtypeStruct((D, D_FF), BF16), jax.ShapeDtypeStruct((D_FF, D), BF16)]
    return pl.pallas_call(
        functools.partial(_ffn_kernel, n_f=n_f, final_norm=final_norm, cast_next=next_f32 is not None),
        grid=(n_tiles, n_f),
        in_specs=in_specs,
        out_specs=out_specs,
        out_shape=out_shape,
        input_output_aliases={} if final_norm else {0: 0},
        compiler_params=_cparams(("arbitrary", "arbitrary")),
        name=f"ffn_l{layer}",
    )(*args)


TB = 256
LOG2E = 1.4426950408889634
DIFF_Q_SCALE = DIFF_D ** -0.5 * LOG2E
MLA_Q_SCALE = MLA_QK ** -0.5 * LOG2E
DIFF_BAND = DIFF_H * DIFF_V
PROJ_SUB = 512


def _diff_proj_kernel(x_ref, g_ref, sh_ref, sc_ref, w_ref, cos_ref, sa_ref, sb_ref,
                      qt_ref, k_ref, vt_ref, h_ref):
    n = pl.program_id(1)

    @pl.when(n == 0)
    def _():
        h_ref[...] = _norm_mod(x_ref[...], g_ref[...], sh_ref[...], sc_ref[...]).astype(BF16)

    def sub_dot(s):
        return jnp.dot(h_ref[...], w_ref[:, s * PROJ_SUB:(s + 1) * PROJ_SUB], preferred_element_type=F32)

    def roped(y):
        cos, sa, sb = cos_ref[...], sa_ref[...], sb_ref[...]
        return jnp.concatenate([_rope(y[:, c * LANES:(c + 1) * LANES], cos, sa, sb, DIFF_ROPE_CH)
                                for c in range(PROJ_SUB // LANES)], axis=1)

    def store_t(ref, s, y):
        for t in range(TM // TB):
            ref[t, s * PROJ_SUB:(s + 1) * PROJ_SUB, :] = y[t * TB:(t + 1) * TB, :].T.astype(BF16)

    @pl.when(n == 0)
    def _():
        for s in range(DIFF_BAND // PROJ_SUB):
            store_t(qt_ref, s, roped(sub_dot(s) * DIFF_Q_SCALE))

    @pl.when(n == 1)
    def _():
        for s in range(DIFF_BAND // PROJ_SUB):
            k_ref[:, s * PROJ_SUB:(s + 1) * PROJ_SUB] = roped(sub_dot(s)).astype(BF16)

    @pl.when(n == 2)
    def _():
        for s in range(DIFF_BAND // PROJ_SUB):
            store_t(vt_ref, s, sub_dot(s))


def _diff_proj(xs, mod, layer, norm_g, w_qkv, tabs):
    const2 = lambda i, n: (0, 0)
    tab_spec = pl.BlockSpec((TM, LANES), lambda i, n: (_rope_tile(i), 0))
    t_spec = pl.BlockSpec((TM // TB, DIFF_BAND, TB), lambda i, n: (i, 0, 0))
    t_shape = jax.ShapeDtypeStruct((T_ALL // TB, DIFF_BAND, TB), BF16)
    return pl.pallas_call(
        _diff_proj_kernel,
        grid=(N_ALL_TILES, 3),
        in_specs=[
            pl.BlockSpec((TM, D), lambda i, n: (i, 0)),
            pl.BlockSpec((1, D), const2),
            _mod_spec(layer, 0), _mod_spec(layer, 1),
            pl.BlockSpec((D, DIFF_BAND), lambda i, n: (0, n)),
            tab_spec, tab_spec, tab_spec,
        ],
        out_specs=[t_spec, pl.BlockSpec((TM, DIFF_BAND), lambda i, n: (i, 0)), t_spec],
        out_shape=[t_shape, jax.ShapeDtypeStruct((T_ALL, DIFF_BAND), BF16), t_shape],
        scratch_shapes=[pltpu.VMEM((TM, D), BF16)],
        compiler_params=_cparams(("arbitrary", "arbitrary")),
        name=f"diff_proj_l{layer}",
    )(xs, norm_g.reshape(1, D), mod, mod, w_qkv, *tabs)


TQ = 1024
TK = 1024
SUM_ROWS = 16


def _scores(k, q_t):
    return jnp.dot(k, q_t, preferred_element_type=F32)


def _attn_update(s, m_blk, v_t_parts, state, first):
    m_ref, l_ref, acc_ref = state
    dv = acc_ref.shape[0]
    m_new = m_blk if first else jnp.maximum(m_ref[...], m_blk)
    pb = jnp.exp2(s - m_new).astype(BF16)
    ones = jnp.ones((SUM_ROWS, TB), BF16)
    pv = None
    for t, v_t in enumerate(v_t_parts):
        d = jnp.dot(jnp.concatenate([v_t, ones], axis=0), pb[t * TB:(t + 1) * TB, :],
                    preferred_element_type=F32)
        pv = d if pv is None else pv + d
    l_blk = pv[dv:dv + 1, :]
    if first:
        l_ref[...] = l_blk
        acc_ref[...] = pv[:dv, :]
    else:
        alpha = jnp.exp2(m_ref[...] - m_new)
        l_ref[...] = alpha * l_ref[...] + l_blk
        acc_ref[...] = alpha * acc_ref[...] + pv[:dv, :]
    m_ref[...] = m_new


def _attn_pipeline(score_fn, v_fn, head_scores, head_v, states, slot_a, slot_b, n):
    def stash(j, slot):
        for s, (s_ref, mb_ref) in zip(score_fn(j), slot):
            s_ref[...] = s
            mb_ref[...] = jnp.max(s, axis=0, keepdims=True)

    def update(j, slot):
        for (s_ref, mb_ref), v_parts, state in zip(slot, v_fn(j), states):
            _attn_update(s_ref[...], mb_ref[...], v_parts, state, False)

    stash(0, slot_a)
    for s, v_parts, state in zip(head_scores, head_v, states):
        _attn_update(s, jnp.max(s, axis=0, keepdims=True), v_parts, state, True)

    def pair(i, carry):
        j = 2 * i
        stash(j + 1, slot_b)
        update(j, slot_a)
        stash(j + 2, slot_a)
        update(j + 1, slot_b)
        return carry

    lax.fori_loop(0, n // 2 - 1, pair, 0)
    stash(n - 1, slot_b)
    update(n - 2, slot_a)
    update(n - 1, slot_b)


def _attn_pipeline3(score_fn, v_fn, head_scores, head_v, states, slots, n):
    ones = jnp.ones((SUM_ROWS, TB), BF16)

    def stage_a(j, parity):
        for sc, (s_ref, mb_ref, _, _) in zip(score_fn(j), slots[parity]):
            s_ref[...] = sc
            mb_ref[...] = jnp.max(sc, axis=0, keepdims=True)

    def stage_b(parity):
        for (s_ref, mb_ref, p_ref, alpha_ref), (m_ref, _, _) in zip(slots[parity], states):
            m_new = jnp.maximum(m_ref[...], mb_ref[...])
            alpha_ref[...] = jnp.exp2(m_ref[...] - m_new)
            p_ref[...] = jnp.exp2(s_ref[...] - m_new).astype(BF16)
            m_ref[...] = m_new

    def stage_c(j, parity):
        for (_, _, p_ref, alpha_ref), v_parts, (_, l_ref, acc_ref) in zip(slots[parity], v_fn(j), states):
            dv = acc_ref.shape[0]
            pv = None
            for t, v_t in enumerate(v_parts):
                d = jnp.dot(jnp.concatenate([v_t, ones], axis=0), p_ref[t * TB:(t + 1) * TB, :],
                            preferred_element_type=F32)
                pv = d if pv is None else pv + d
            alpha = alpha_ref[...]
            l_ref[...] = alpha * l_ref[...] + pv[dv:dv + 1, :]
            acc_ref[...] = alpha * acc_ref[...] + pv[:dv, :]

    stage_a(0, 0)
    for sc, v_parts, state in zip(head_scores, head_v, states):
        _attn_update(sc, jnp.max(sc, axis=0, keepdims=True), v_parts, state, True)
    stage_a(1, 1)
    stage_b(0)

    def pair(i, carry):
        j = 2 * i
        stage_a(j + 2, 0)
        stage_b(1)
        stage_c(j, 0)
        stage_a(j + 3, 1)
        stage_b(0)
        stage_c(j + 1, 1)
        return carry

    lax.fori_loop(0, n // 2 - 1, pair, 0)
    stage_b(1)
    stage_c(n - 2, 0)
    stage_c(n - 1, 1)


def _load_q_t(qt_ref, rows):
    return jnp.concatenate([qt_ref[t, rows, :] for t in range(qt_ref.shape[0])], axis=1)


def _lat_chunk_fns(k_ref, vt_ref, v_rows, tk=TK):
    k_fn = lambda j: k_ref[pl.ds(pl.multiple_of(j * tk, tk), tk), :]
    v_fn = lambda j: [[vt_ref[(tk // TB) * j + t, rows, :] for t in range(tk // TB)] for rows in v_rows]
    return k_fn, v_fn


def _slot_scratch3(n_chain, tk, tq):
    return [pltpu.VMEM((tk, tq), F32), pltpu.VMEM((1, tq), F32),
            pltpu.VMEM((tk, tq), BF16), pltpu.VMEM((1, tq), F32)] * (2 * n_chain)


def _split_slots3(refs, n_chain):
    quads = [tuple(refs[4 * i:4 * i + 4]) for i in range(2 * n_chain)]
    return quads[:n_chain], quads[n_chain:]


def _slot_scratch(n_chain, tq):
    return [pltpu.VMEM((TK, tq), F32), pltpu.VMEM((1, tq), F32)] * (2 * n_chain)


def _split_slots(refs, n_chain):
    pairs = [(refs[2 * i], refs[2 * i + 1]) for i in range(2 * n_chain)]
    return pairs[:n_chain], pairs[n_chain:]


def _diff_attn_kernel(*refs, has_lat, lambda_init):
    if has_lat:
        qt_ref, kc_ref, vtc_ref, k_ref, vt_ref, lam_ref, sg_ref, o_ref = refs[:8]
        scratch = refs[8:]
    else:
        qt_ref, kc_ref, vtc_ref, lam_ref, sg_ref, o_ref = refs[:6]
        scratch = refs[6:]
    states = (scratch[0:3], scratch[3:6])
    halves = (slice(0, DIFF_D), slice(DIFF_D, 2 * DIFF_D))

    def scores_of(k):
        return [_scores(k[:, halves[idx]], _load_q_t(qt_ref, halves[idx])) for idx in range(2)]

    ctx_scores = scores_of(kc_ref[...])
    v_rows = [slice(0, DIFF_V)] * 2
    ctx_v = [[vtc_ref[0]]] * 2
    if has_lat:
        k_fn, v_fn = _lat_chunk_fns(k_ref, vt_ref, v_rows)
        slot_a, slot_b = _split_slots(scratch[6:], 2)
        _attn_pipeline(lambda j: scores_of(k_fn(j)), v_fn, ctx_scores, ctx_v, states,
                       slot_a, slot_b, SEQ // TK)
    else:
        for s, v_parts, state in zip(ctx_scores, ctx_v, states):
            _attn_update(s, jnp.max(s, axis=0, keepdims=True), v_parts, state, True)

    lam_v = lam_ref[...]
    lam = (jnp.exp(jnp.sum(lam_v[0:1] * lam_v[1:2], axis=-1, keepdims=True))
           - jnp.exp(jnp.sum(lam_v[2:3] * lam_v[3:4], axis=-1, keepdims=True)) + lambda_init)
    (_, l0, a0), (_, l1, a1) = states
    o_t = a0[...] * (1.0 / l0[...]) - lam * (a1[...] * (1.0 / l1[...]))
    o = o_t.T
    o = o * lax.rsqrt(jnp.mean(o * o, axis=-1, keepdims=True) + EPS) * sg_ref[...]
    o_ref[...] = (o * (1.0 - lambda_init)).astype(BF16)


def _diff_attn(q_t, k, v_t, lam_tab, subln_g, lambda_init, has_lat):
    ctx_blk0 = T_LAT // CTX
    if has_lat:
        tq = TQ
        grid = (BATCH, DIFF_H, SEQ // tq)
        q_blk = lambda b, h, i: b * (SEQ // tq) + i
        o_blk = q_blk
    else:
        tq = CTX
        grid = (BATCH, DIFF_H, 1)
        q_blk = lambda b, h, i: ctx_blk0 + b
        o_blk = lambda b, h, i: b
    in_specs = [
        pl.BlockSpec((tq // TB, DIFF_V, TB), lambda b, h, i: (q_blk(b, h, i), h, 0)),
        pl.BlockSpec((CTX, DIFF_V), lambda b, h, i: (ctx_blk0 + b, h)),
        pl.BlockSpec((1, DIFF_V, TB), lambda b, h, i: (ctx_blk0 + b, h, 0)),
    ]
    args = [q_t, k, v_t]
    if has_lat:
        in_specs += [
            pl.BlockSpec((SEQ, DIFF_V), lambda b, h, i: (b, h)),
            pl.BlockSpec((SEQ // TB, DIFF_V, TB), lambda b, h, i: (b, h, 0)),
        ]
        args += [k, v_t]
    in_specs += [
        pl.BlockSpec((8, DIFF_D), lambda b, h, i: (0, 0)),
        pl.BlockSpec((1, DIFF_V), lambda b, h, i: (0, 0)),
    ]
    args += [lam_tab, subln_g.reshape(1, DIFF_V)]
    return pl.pallas_call(
        functools.partial(_diff_attn_kernel, has_lat=has_lat, lambda_init=lambda_init),
        grid=grid,
        in_specs=in_specs,
        out_specs=pl.BlockSpec((tq, DIFF_V), lambda b, h, i: (o_blk(b, h, i), h)),
        out_shape=jax.ShapeDtypeStruct((T_LAT if has_lat else T_CTX, DIFF_H * DIFF_V), BF16),
        scratch_shapes=([pltpu.VMEM((1, tq), F32), pltpu.VMEM((1, tq), F32), pltpu.VMEM((DIFF_V, tq), F32)] * 2
                        + (_slot_scratch(2, tq) if has_lat else [])),
        compiler_params=_cparams(("arbitrary",) * 3),
        name="diff_attn_lat" if has_lat else "diff_attn_ctx",
    )(*args)


MLA_HPS = 2
MLA_TQ = 1024
MLA_TK = 512


def _mla_attn_kernel(qt_ref, kc_ref, vtc_ref, k_ref, vt_ref, o_ref, *scratch):
    states = [scratch[3 * i:3 * i + 3] for i in range(MLA_HPS)]
    qk = [slice(i * MLA_QK_PAD, (i + 1) * MLA_QK_PAD) for i in range(MLA_HPS)]
    v_rows = [slice(i * MLA_V, (i + 1) * MLA_V) for i in range(MLA_HPS)]

    def scores_of(k):
        return [_scores(k[:, qk[i]], _load_q_t(qt_ref, qk[i])) for i in range(MLA_HPS)]

    k_fn, v_fn = _lat_chunk_fns(k_ref, vt_ref, v_rows, MLA_TK)
    _attn_pipeline3(lambda j: scores_of(k_fn(j)), v_fn, scores_of(kc_ref[...]),
                    [[vtc_ref[0, rows, :]] for rows in v_rows], states,
                    _split_slots3(scratch[3 * MLA_HPS:], MLA_HPS), SEQ // MLA_TK)
    for i, (_, l_ref, acc_ref) in enumerate(states):
        o_ref[:, v_rows[i]] = (acc_ref[...] * (1.0 / l_ref[...])).T.astype(BF16)


def _mla_attn(q_t, k, v_t):
    ctx_blk0 = T_LAT // CTX
    qk_w, v_w = MLA_HPS * MLA_QK_PAD, MLA_HPS * MLA_V
    return pl.pallas_call(
        _mla_attn_kernel,
        grid=(BATCH, MLA_H // MLA_HPS, SEQ // MLA_TQ),
        in_specs=[
            pl.BlockSpec((MLA_TQ // TB, qk_w, TB), lambda b, h, i: (b * (SEQ // MLA_TQ) + i, h, 0)),
            pl.BlockSpec((CTX, qk_w), lambda b, h, i: (ctx_blk0 + b, h)),
            pl.BlockSpec((1, v_w, TB), lambda b, h, i: (ctx_blk0 + b, h, 0)),
            pl.BlockSpec((SEQ, qk_w), lambda b, h, i: (b, h)),
            pl.BlockSpec((SEQ // TB, v_w, TB), lambda b, h, i: (b, h, 0)),
        ],
        out_specs=pl.BlockSpec((MLA_TQ, v_w), lambda b, h, i: (b * (SEQ // MLA_TQ) + i, h)),
        out_shape=jax.ShapeDtypeStruct((T_LAT, MLA_H * MLA_V), BF16),
        scratch_shapes=([pltpu.VMEM((1, MLA_TQ), F32), pltpu.VMEM((1, MLA_TQ), F32), pltpu.VMEM((MLA_V, MLA_TQ), F32)]
                        * MLA_HPS + _slot_scratch3(MLA_HPS, MLA_TK, MLA_TQ)),
        compiler_params=_cparams(("arbitrary",) * 3),
        name="mla_attn",
    )(q_t, k, v_t, k, v_t)


def _mla_proj_kernel(x_ref, g_ref, sh_ref, sc_ref, wd_ref, qg_ref, kvg_ref, wuq_ref, wukv_ref,
                     cos_ref, sa_ref, sb_ref, qt_ref, k_ref, vt_ref):
    h = _norm_mod(x_ref[...], g_ref[...], sh_ref[...], sc_ref[...]).astype(BF16)
    c = jnp.dot(h, wd_ref[...], preferred_element_type=F32)
    cq = c[:, :MLA_QR_PAD]
    ckv = c[:, MLA_QR_PAD:MLA_QR_PAD + MLA_KVR]
    kr = c[:, MLA_QR_PAD + MLA_KVR:]
    cqn = cq * lax.rsqrt(jnp.sum(cq * cq, axis=-1, keepdims=True) * (1.0 / MLA_QR) + EPS) * qg_ref[...]
    ckvn = ckv * lax.rsqrt(jnp.mean(ckv * ckv, axis=-1, keepdims=True) + EPS) * kvg_ref[...]
    cos, sa, sb = cos_ref[...], sa_ref[...], sb_ref[...]
    kr_rot = _rope(kr, cos, sa, sb, MLA_ROPE_CH).astype(BF16)
    q = jnp.dot(cqn.astype(BF16), wuq_ref[...], preferred_element_type=F32) * MLA_Q_SCALE
    kv = jnp.dot(ckvn.astype(BF16), wukv_ref[...], preferred_element_type=F32)
    for hd in range(MLA_H):
        base = hd * MLA_QK_PAD
        nope, rope = slice(base, base + MLA_NOPE), slice(base + MLA_NOPE, base + MLA_QK_PAD)
        qt_ref[0, nope, :] = q[:, nope].T.astype(BF16)
        qt_ref[0, rope, :] = _rope(q[:, rope], cos, sa, sb, MLA_ROPE_CH).T.astype(BF16)
        k_ref[:, nope] = kv[:, hd * MLA_NOPE:(hd + 1) * MLA_NOPE].astype(BF16)
        k_ref[:, rope] = kr_rot
        vcols = slice(MLA_H * MLA_NOPE + hd * MLA_V, MLA_H * MLA_NOPE + (hd + 1) * MLA_V)
        vt_ref[0, hd * MLA_V:(hd + 1) * MLA_V, :] = kv[:, vcols].T.astype(BF16)


def _mla_proj(xs, mod, layer, norm_g, w_d, q_g, kv_g, w_uq, w_ukv, tabs):
    tm = TB
    scale = TM // tm
    const2 = lambda i: (0, 0)
    mod_spec = lambda k: pl.BlockSpec((None, None, 1, D), lambda i: (layer, _mod_row(i // scale), 0, k))
    tab_spec = pl.BlockSpec((tm, LANES), lambda i: (
        jnp.where(i < N_LAT_TILES * scale, i % (TILES_PER_BATCH * scale), TILES_PER_BATCH * scale), 0))
    n_q = MLA_H * MLA_QK_PAD
    return pl.pallas_call(
        _mla_proj_kernel,
        grid=(N_ALL_TILES * scale,),
        in_specs=[
            pl.BlockSpec((tm, D), lambda i: (i, 0)),
            pl.BlockSpec((1, D), const2),
            mod_spec(0), mod_spec(1),
            pl.BlockSpec((D, MLA_C_PAD), const2),
            pl.BlockSpec((1, MLA_QR_PAD), const2),
            pl.BlockSpec((1, MLA_KVR), const2),
            pl.BlockSpec((MLA_QR_PAD, n_q), const2),
            pl.BlockSpec((MLA_KVR, MLA_H * (MLA_NOPE + MLA_V)), const2),
            tab_spec, tab_spec, tab_spec,
        ],
        out_specs=[
            pl.BlockSpec((1, n_q, tm), lambda i: (i, 0, 0)),
            pl.BlockSpec((tm, n_q), lambda i: (i, 0)),
            pl.BlockSpec((1, MLA_H * MLA_V, tm), lambda i: (i, 0, 0)),
        ],
        out_shape=[
            jax.ShapeDtypeStruct((T_ALL // tm, n_q, tm), BF16),
            jax.ShapeDtypeStruct((T_ALL, n_q), BF16),
            jax.ShapeDtypeStruct((T_ALL // tm, MLA_H * MLA_V, tm), BF16),
        ],
        compiler_params=_cparams(("arbitrary",)),
        name=f"mla_proj_l{layer}",
    )(xs, norm_g.reshape(1, D), mod, mod, w_d, q_g, kv_g, w_uq, w_ukv, *tabs)


def _rope_tables(rot_dim):
    a = rot_dim // 2
    ch = a // 2
    n_rows = SEQ // GRID_W
    inv = ROPE_BASE ** (-jnp.arange(0, a, 2, dtype=F32) / a)

    def axis_angles(n):
        ang = jnp.arange(n, dtype=jnp.int32).astype(F32)[:, None] * inv[None, :]
        return jnp.concatenate([ang, ang], axis=-1)

    def grid(fn):
        by_row = jnp.broadcast_to(fn(axis_angles(n_rows))[:, None, :], (n_rows, GRID_W, a))
        by_col = jnp.broadcast_to(fn(axis_angles(GRID_W))[None, :, :], (n_rows, GRID_W, a))
        return jnp.concatenate([by_row, by_col], axis=-1).reshape(SEQ, rot_dim)

    cos, sin = grid(jnp.cos), grid(jnp.sin)
    lane = jnp.arange(rot_dim)
    first = (lane % a) < ch
    sin_a = jnp.where(first[None, :], -sin, 0.0)
    sin_b = jnp.where(first[None, :], 0.0, sin)
    pad = LANES - rot_dim
    cos = jnp.pad(cos, ((0, TM), (0, pad)), constant_values=1.0)
    cos = cos.at[SEQ:, :].set(1.0)
    sin_a = jnp.pad(sin_a, ((0, TM), (0, pad)))
    sin_b = jnp.pad(sin_b, ((0, TM), (0, pad)))
    return cos, sin_a, sin_b


def kernel(x, c, ctx, c_ctx, w_mod, b_mod, norm1_g, norm2_g, w_ff1, w_ff2, a_w_in, a_b_in, a_ln_g, a_w_s, a_b_s, a_w_out, b_w_qkv, b_lam_q1, b_lam_k1, b_lam_q2, b_lam_k2, b_subln_g, b_w_o, c_w_dqkv, c_q_norm_g, c_w_uq, c_kv_norm_g, c_w_ukv, c_w_o, final_g):
    xs = (x.reshape(T_LAT, D), ctx.reshape(T_CTX, D))
    cc = jnp.concatenate([c, c_ctx[None, :], jnp.zeros((8 - BATCH - 1, D), F32)], axis=0)
    mod = _modulation(cc, w_mod, b_mod).reshape(DEPTH, 8, 1, N_MOD * D)

    tabs_b = _rope_tables(DIFF_D)
    tabs_c = _rope_tables(MLA_ROPE)
    w1_b, w2_b = w_ff1[0].astype(BF16), w_ff2[0].astype(BF16)

    out = None
    for l in range(DEPTH):
        kind, idx = l % 3, l // 3
        ctx_later = any(j % 3 != 0 for j in range(l + 1, DEPTH))
        last = l == DEPTH - 1
        n_tiles = N_ALL_TILES if ctx_later else N_LAT_TILES
        if kind == 0:
            b_s = jnp.broadcast_to(a_b_s[idx][:, :, None], (GROUPS, CHUNK, CHUNK))
            t = _gmlp_mix(xs, mod, l, n_tiles, norm1_g[l], a_w_in[idx].astype(BF16), a_b_in[idx],
                          a_ln_g[idx], a_w_s[idx].astype(BF16), b_s)
            xs, h2 = _out_proj(xs, t, a_w_out[idx].astype(BF16), mod, l, n_tiles, norm2_g[l])
        elif kind == 1:
            lambda_init = 0.8 - 0.6 * math.exp(-0.3 * l)
            q_t, k, v_t = _diff_proj(xs, mod, l, norm1_g[l], b_w_qkv[idx].astype(BF16), tabs_b)
            lam_tab = jnp.concatenate([b_lam_q1[idx][None], b_lam_k1[idx][None], b_lam_q2[idx][None],
                                       b_lam_k2[idx][None], jnp.zeros((4, DIFF_D), F32)], axis=0)
            o = _diff_attn(q_t, k, v_t, lam_tab, b_subln_g[idx], lambda_init, True)
            if ctx_later:
                o = (o, _diff_attn(q_t, k, v_t, lam_tab, b_subln_g[idx], lambda_init, False))
            xs, h2 = _out_proj(xs, o, b_w_o[idx].astype(BF16), mod, l, n_tiles, norm2_g[l])
        else:
            w_d = c_w_dqkv[idx]
            w_d = jnp.concatenate([
                jnp.pad(w_d[:, :MLA_QR], ((0, 0), (0, MLA_QR_PAD - MLA_QR))),
                w_d[:, MLA_QR:MLA_QR + MLA_KVR],
                jnp.pad(w_d[:, MLA_QR + MLA_KVR:], ((0, 0), (0, LANES - MLA_ROPE)))], axis=1).astype(BF16)
            w_uq = jnp.pad(c_w_uq[idx].reshape(MLA_QR, MLA_H, MLA_QK),
                           ((0, MLA_QR_PAD - MLA_QR), (0, 0), (0, MLA_QK_PAD - MLA_QK)))
            w_uq = w_uq.reshape(MLA_QR_PAD, MLA_H * MLA_QK_PAD).astype(BF16)
            w_ukv = c_w_ukv[idx].reshape(MLA_KVR, MLA_H, MLA_NOPE + MLA_V)
            w_ukv = jnp.concatenate([w_ukv[:, :, :MLA_NOPE].reshape(MLA_KVR, MLA_H * MLA_NOPE),
                                     w_ukv[:, :, MLA_NOPE:].reshape(MLA_KVR, MLA_H * MLA_V)],
                                    axis=1).astype(BF16)
            q_g = jnp.pad(c_q_norm_g[idx], (0, MLA_QR_PAD - MLA_QR)).reshape(1, MLA_QR_PAD)
            q_t, k, v_t = _mla_proj(xs, mod, l, norm1_g[l], w_d, q_g, c_kv_norm_g[idx].reshape(1, MLA_KVR),
                                    w_uq, w_ukv, tabs_c)
            o = _mla_attn(q_t, k, v_t)
            xs, h2 = _out_proj(xs, o, c_w_o[idx].astype(BF16), mod, l, N_LAT_TILES, norm2_g[l])
        if ctx_later and kind == 2:
            raise NotImplementedError("context update after an MLA layer is not needed at this depth")
        res = _ffn(xs, h2, mod, l, n_tiles, w1_b, w2_b, final_g, last, None if last else (w_ff1, w_ff2))
        if last:
            out, = res
        else:
            xs, w1_b, w2_b = res
    return out.reshape(BATCH, SEQ, D)
```

```python
import functools
import math

import jax
import jax.numpy as jnp
from jax import lax
from jax.experimental import pallas as pl
from jax.experimental.pallas import tpu as pltpu

D = 2048
BATCH = 2
SEQ = 8192
DEPTH = 4
GRID_W = 64
CTX = 256
N_MOD = 6
D_FF = 4 * D
EPS = 1e-6
ROPE_BASE = 10000.0

CHUNK = 128
GROUPS = 16
GDIM = 128

DIFF_H = 8
DIFF_D = 128
DIFF_V = 256

MLA_H = 16
MLA_QR = 448
MLA_QR_PAD = 512
MLA_KVR = 512
MLA_NOPE = 128
MLA_ROPE = 64
MLA_V = 128
MLA_QK = MLA_NOPE + MLA_ROPE
MLA_QK_PAD = 256
MLA_C_PAD = MLA_QR_PAD + MLA_KVR + 128

T_LAT = BATCH * SEQ
T_CTX = BATCH * CTX
T_ALL = T_LAT + T_CTX

TM = 512
TILES_PER_BATCH = SEQ // TM
N_LAT_TILES = T_LAT // TM
N_ALL_TILES = T_ALL // TM
ROPE_ID_TILE = SEQ // TM

VMEM_LIMIT = 56 * 1024 * 1024

LANES = 128
DIFF_ROPE_CH = DIFF_D // 4
MLA_ROPE_CH = MLA_ROPE // 4

BF16 = jnp.bfloat16
F32 = jnp.float32


def _cparams(sem):
    return pltpu.CompilerParams(dimension_semantics=sem, vmem_limit_bytes=VMEM_LIMIT)


def _mod_row(i):
    return jnp.minimum(i // TILES_PER_BATCH, 2)


def _mod_spec(layer, k):
    return pl.BlockSpec((None, None, 1, D), lambda i, *_: (layer, _mod_row(i), 0, k))


def _rope_tile(i):
    return jnp.where(i < N_LAT_TILES, i % TILES_PER_BATCH, ROPE_ID_TILE)


def _norm_mod(x, g, sh, sc):
    y = x * lax.rsqrt(jnp.mean(x * x, axis=-1, keepdims=True) + EPS)
    y = y * g
    return y * (1.0 + sc) + sh


def _token_specs(xs):
    if isinstance(xs, tuple):
        width = xs[0].shape[1]
        return ([pl.BlockSpec((TM, width), lambda i, *_: (jnp.minimum(i, N_LAT_TILES - 1), 0)),
                 pl.BlockSpec((TM, width), lambda i, *_: (jnp.maximum(i - N_LAT_TILES, 0), 0))], list(xs))
    return [pl.BlockSpec((TM, xs.shape[1]), lambda i, *_: (i, 0))], [xs]


def _token_tile(src_refs):
    if len(src_refs) == 2:
        return jnp.where(pl.program_id(0) < N_LAT_TILES, src_refs[0][...], src_refs[1][...])
    return src_refs[0][...]


def _rope(x, cos, sin_a, sin_b, ch):
    return (x * cos + pltpu.roll(x, LANES - ch, 1) * sin_a + pltpu.roll(x, ch, 1) * sin_b)


def _mod_kernel(c_ref, w_ref, b_ref, o_ref):
    c = c_ref[...]
    s = (c * jax.nn.sigmoid(c)).astype(BF16)
    o_ref[...] = jnp.dot(s, w_ref[...].astype(BF16), preferred_element_type=F32) + b_ref[...]


def _modulation(cc, w_mod, b_mod):
    tn = 1024
    n = N_MOD * D
    return pl.pallas_call(
        _mod_kernel,
        grid=(DEPTH, n // tn),
        in_specs=[
            pl.BlockSpec((8, D), lambda l, j: (0, 0)),
            pl.BlockSpec((None, D, tn), lambda l, j: (l, 0, j)),
            pl.BlockSpec((None, 1, tn), lambda l, j: (l, 0, j)),
        ],
        out_specs=pl.BlockSpec((None, 8, tn), lambda l, j: (l, 0, j)),
        out_shape=jax.ShapeDtypeStruct((DEPTH, 8, n), F32),
        compiler_params=_cparams(("arbitrary", "arbitrary")),
        name="modulation",
    )(cc, w_mod, b_mod.reshape(DEPTH, 1, n))


def _gelu_tanh(x):
    return 0.5 * x * (1.0 + jnp.tanh(math.sqrt(2.0 / math.pi) * (x + 0.044715 * (x * x * x))))


def _gmlp_kernel(*refs, n_src):
    src = refs[:n_src]
    g_ref, sh_ref, sc_ref, win_ref, bin_ref, lng_ref, ws_ref, bs_ref, o_ref, vn_ref = refs[n_src:]
    nchunk = TM // CHUNK
    h = _norm_mod(_token_tile(src), g_ref[...], sh_ref[...], sc_ref[...]).astype(BF16)
    zv = _gelu_tanh(jnp.dot(h, win_ref[:, D:], preferred_element_type=F32) + bin_ref[:, D:])
    mu = jnp.mean(zv, axis=-1, keepdims=True)
    zc = zv - mu
    vn = zc * lax.rsqrt(jnp.mean(zc * zc, axis=-1, keepdims=True) + EPS) * lng_ref[...]
    vn_ref[...] = vn.astype(BF16)
    ublk = 512
    zus = [_gelu_tanh(jnp.dot(h, win_ref[:, ub * ublk:(ub + 1) * ublk], preferred_element_type=F32)
                      + bin_ref[:, ub * ublk:(ub + 1) * ublk]) for ub in range(D // ublk)]
    for ub, zu in enumerate(zus):
        for gg in range(ublk // GDIM):
            grp = ub * (ublk // GDIM) + gg
            cols = slice(grp * GDIM, (grp + 1) * GDIM)
            rhs = jnp.concatenate([vn_ref[c * CHUNK:(c + 1) * CHUNK, cols] for c in range(nchunk)], axis=1)
            sv = jnp.dot(ws_ref[grp], rhs, preferred_element_type=F32)
            for c in range(nchunk):
                svc = sv[:, c * CHUNK:(c + 1) * CHUNK] + bs_ref[grp]
                u = zu[c * CHUNK:(c + 1) * CHUNK, gg * GDIM:(gg + 1) * GDIM]
                o_ref[c * CHUNK:(c + 1) * CHUNK, cols] = (u * svc).astype(BF16)


def _gmlp_mix(xs, mod, layer, n_tiles, norm_g, w_in, b_in, ln_g, w_s, b_s):
    const2 = lambda i: (0, 0)
    src_specs, src_args = _token_specs(xs)
    return pl.pallas_call(
        functools.partial(_gmlp_kernel, n_src=len(src_args)),
        grid=(n_tiles,),
        in_specs=src_specs + [
            pl.BlockSpec((1, D), const2),
            _mod_spec(layer, 0), _mod_spec(layer, 1),
            pl.BlockSpec((D, 2 * D), const2),
            pl.BlockSpec((1, 2 * D), const2),
            pl.BlockSpec((1, D), const2),
            pl.BlockSpec((GROUPS, CHUNK, CHUNK), lambda i: (0, 0, 0)),
            pl.BlockSpec((GROUPS, CHUNK, CHUNK), lambda i: (0, 0, 0)),
        ],
        out_specs=pl.BlockSpec((TM, D), lambda i: (i, 0)),
        out_shape=jax.ShapeDtypeStruct((n_tiles * TM, D), BF16),
        scratch_shapes=[pltpu.VMEM((TM, D), BF16)],
        compiler_params=_cparams(("arbitrary",)),
        name=f"gmlp_mix_l{layer}",
    )(*src_args, norm_g.reshape(1, D), mod, mod, w_in, b_in.reshape(1, 2 * D), ln_g.reshape(1, D), w_s, b_s)


def _out_proj_kernel(*refs, n_src, n_act):
    src = refs[:n_src]
    act = refs[n_src:n_src + n_act]
    w_ref, gate_ref, g_ref, sh_ref, sc_ref, o_ref, h_ref = refs[n_src + n_act:]
    y = jnp.dot(_token_tile(act), w_ref[...], preferred_element_type=F32)
    x1 = _token_tile(src) + gate_ref[...] * y
    o_ref[...] = x1
    h_ref[...] = _norm_mod(x1, g_ref[...], sh_ref[...], sc_ref[...]).astype(BF16)


def _out_proj(xs, a, w, mod, layer, n_tiles, ffn_norm_g):
    src_specs, src_args = _token_specs(xs)
    act_specs, act_args = _token_specs(a)
    tile = pl.BlockSpec((TM, D), lambda i: (i, 0))
    return pl.pallas_call(
        functools.partial(_out_proj_kernel, n_src=len(src_args), n_act=len(act_args)),
        grid=(n_tiles,),
        in_specs=src_specs + act_specs + [
            pl.BlockSpec((w.shape[0], D), lambda i: (0, 0)),
            _mod_spec(layer, 2),
            pl.BlockSpec((1, D), lambda i: (0, 0)),
            _mod_spec(layer, 3), _mod_spec(layer, 4),
        ],
        out_specs=[tile, tile],
        out_shape=[jax.ShapeDtypeStruct((T_ALL, D), F32), jax.ShapeDtypeStruct((n_tiles * TM, D), BF16)],
        input_output_aliases={0: 0} if len(src_args) == 1 else {},
        compiler_params=_cparams(("arbitrary",)),
        name=f"out_proj_l{layer}",
    )(*src_args, *act_args, w, mod, ffn_norm_g.reshape(1, D), mod, mod)


CAST_BLOCKS = 32
FFN_TF = 1024
FFN_CHUNKS = D_FF // FFN_TF
FFN_DMA_SPLIT = 4


def _ffn_kernel(*refs, n_tiles, final_norm, cast_next):
    x_ref, h_ref, gate_ref, w1_hbm, w2_hbm, fg_ref = refs[:6]
    if cast_next:
        wn1_ref, wn2_ref, o_ref, wb1_ref, wb2_ref, w1_buf, w2_buf, sem = refs[6:]
    else:
        o_ref, w1_buf, w2_buf, sem = refs[6:]
    i = pl.program_id(0)

    def copies(f):
        slot = f % 2
        r1, r2 = D // FFN_DMA_SPLIT, FFN_TF // FFN_DMA_SPLIT
        out = []
        for p in range(FFN_DMA_SPLIT):
            out.append(pltpu.make_async_copy(w1_hbm.at[pl.ds(p * r1, r1), pl.ds(f * FFN_TF, FFN_TF)],
                                             w1_buf.at[slot, pl.ds(p * r1, r1), :], sem.at[0, slot, p]))
            out.append(pltpu.make_async_copy(w2_hbm.at[pl.ds(f * FFN_TF + p * r2, r2), :],
                                             w2_buf.at[slot, pl.ds(p * r2, r2), :], sem.at[1, slot, p]))
        return out

    @pl.when(i == 0)
    def _():
        for c in copies(0):
            c.start()

    h = h_ref[...]
    for f in range(FFN_CHUNKS):
        for c in copies((f + 1) % FFN_CHUNKS):
            c.start()
        for c in copies(f):
            c.wait()
        a = jnp.dot(h, w1_buf[f % 2], preferred_element_type=F32)
        a = jnp.square(jnp.maximum(a, 0.0)).astype(BF16)
        y = jnp.dot(a, w2_buf[f % 2], preferred_element_type=F32)
        if f == 0:
            o_ref[...] = y
        else:
            o_ref[...] += y
    if cast_next:
        wb1_ref[...] = wn1_ref[...].astype(BF16)
        wb2_ref[...] = wn2_ref[...].astype(BF16)
    y = x_ref[...] + gate_ref[...] * o_ref[...]
    if final_norm:
        y = y * lax.rsqrt(jnp.mean(y * y, axis=-1, keepdims=True) + EPS) * fg_ref[...]
    o_ref[...] = y

    @pl.when(i == n_tiles - 1)
    def _():
        for c in copies(0):
            c.wait()


def _ffn(xs, h, mod, layer, n_tiles, w1, w2, final_g, final_norm, next_f32=None):
    rows = T_LAT if final_norm else T_ALL
    tile = pl.BlockSpec((TM, D), lambda i: (i, 0))
    hbm = pl.BlockSpec(memory_space=pl.ANY)
    in_specs = [
        tile, tile,
        _mod_spec(layer, 5),
        hbm, hbm,
        pl.BlockSpec((1, D), lambda i: (0, 0)),
    ]
    args = [xs, h, mod, w1, w2, final_g.reshape(1, D)]
    out_specs = [tile]
    out_shape = [jax.ShapeDtypeStruct((rows, D), F32)]
    if next_f32 is not None:
        assert n_tiles >= CAST_BLOCKS
        band = lambda i: (i * CAST_BLOCKS) // n_tiles
        r1, r2 = D // CAST_BLOCKS, D_FF // CAST_BLOCKS
        in_specs += [pl.BlockSpec((None, r1, D_FF), lambda i: (layer + 1, band(i), 0)),
                     pl.BlockSpec((None, r2, D), lambda i: (layer + 1, band(i), 0))]
        args += list(next_f32)
        out_specs += [pl.BlockSpec((r1, D_FF), lambda i: (band(i), 0)),
                      pl.BlockSpec((r2, D), lambda i: (band(i), 0))]
        out_shape += [jax.ShapeDtypeStruct((D, D_FF), BF16), jax.ShapeDtypeStruct((D_FF, D), BF16)]
    return pl.pallas_call(
        functools.partial(_ffn_kernel, n_tiles=n_tiles, final_norm=final_norm, cast_next=next_f32 is not None),
        grid=(n_tiles,),
        in_specs=in_specs,
        out_specs=out_specs,
        out_shape=out_shape,
        scratch_shapes=[pltpu.VMEM((2, D, FFN_TF), BF16), pltpu.VMEM((2, FFN_TF, D), BF16),
                        pltpu.SemaphoreType.DMA((2, 2, FFN_DMA_SPLIT))],
        input_output_aliases={} if final_norm else {0: 0},
        compiler_params=_cparams(("arbitrary",)),
        name=f"ffn_l{layer}",
    )(*args)


TB = 256
LOG2E = 1.4426950408889634
DIFF_Q_SCALE = DIFF_D ** -0.5 * LOG2E
MLA_Q_SCALE = MLA_QK ** -0.5 * LOG2E
DIFF_BAND = DIFF_H * DIFF_V
PROJ_SUB = 512


def _diff_proj_kernel(x_ref, g_ref, sh_ref, sc_ref, w_ref, cos_ref, sa_ref, sb_ref,
                      qt_ref, k_ref, vt_ref, h_ref):
    n = pl.program_id(1)

    @pl.when(n == 0)
    def _():
        h_ref[...] = _norm_mod(x_ref[...], g_ref[...], sh_ref[...], sc_ref[...]).astype(BF16)

    def sub_dot(s):
        return jnp.dot(h_ref[...], w_ref[:, s * PROJ_SUB:(s + 1) * PROJ_SUB], preferred_element_type=F32)

    def roped(y):
        cos, sa, sb = cos_ref[...], sa_ref[...], sb_ref[...]
        return jnp.concatenate([_rope(y[:, c * LANES:(c + 1) * LANES], cos, sa, sb, DIFF_ROPE_CH)
                                for c in range(PROJ_SUB // LANES)], axis=1)

    def store_t(ref, s, y):
        for t in range(TM // TB):
            ref[t, s * PROJ_SUB:(s + 1) * PROJ_SUB, :] = y[t * TB:(t + 1) * TB, :].T.astype(BF16)

    @pl.when(n == 0)
    def _():
        for s in range(DIFF_BAND // PROJ_SUB):
            store_t(qt_ref, s, roped(sub_dot(s) * DIFF_Q_SCALE))

    @pl.when(n == 1)
    def _():
        for s in range(DIFF_BAND // PROJ_SUB):
            k_ref[:, s * PROJ_SUB:(s + 1) * PROJ_SUB] = roped(sub_dot(s)).astype(BF16)

    @pl.when(n == 2)
    def _():
        for s in range(DIFF_BAND // PROJ_SUB):
            store_t(vt_ref, s, sub_dot(s))


def _diff_proj(xs, mod, layer, norm_g, w_qkv, tabs):
    const2 = lambda i, n: (0, 0)
    tab_spec = pl.BlockSpec((TM, LANES), lambda i, n: (_rope_tile(i), 0))
    t_spec = pl.BlockSpec((TM // TB, DIFF_BAND, TB), lambda i, n: (i, 0, 0))
    t_shape = jax.ShapeDtypeStruct((T_ALL // TB, DIFF_BAND, TB), BF16)
    return pl.pallas_call(
        _diff_proj_kernel,
        grid=(N_ALL_TILES, 3),
        in_specs=[
            pl.BlockSpec((TM, D), lambda i, n: (i, 0)),
            pl.BlockSpec((1, D), const2),
            _mod_spec(layer, 0), _mod_spec(layer, 1),
            pl.BlockSpec((D, DIFF_BAND), lambda i, n: (0, n)),
            tab_spec, tab_spec, tab_spec,
        ],
        out_specs=[t_spec, pl.BlockSpec((TM, DIFF_BAND), lambda i, n: (i, 0)), t_spec],
        out_shape=[t_shape, jax.ShapeDtypeStruct((T_ALL, DIFF_BAND), BF16), t_shape],
        scratch_shapes=[pltpu.VMEM((TM, D), BF16)],
        compiler_params=_cparams(("arbitrary", "arbitrary")),
        name=f"diff_proj_l{layer}",
    )(xs, norm_g.reshape(1, D), mod, mod, w_qkv, *tabs)


TQ = 1024
TK = 1024
SUM_ROWS = 16


def _scores(k, q_t):
    return jnp.dot(k, q_t, preferred_element_type=F32)


def _attn_update(s, m_blk, v_t_parts, state, first):
    m_ref, l_ref, acc_ref = state
    dv = acc_ref.shape[0]
    m_new = m_blk if first else jnp.maximum(m_ref[...], m_blk)
    pb = jnp.exp2(s - m_new).astype(BF16)
    ones = jnp.ones((SUM_ROWS, TB), BF16)
    pv = None
    for t, v_t in enumerate(v_t_parts):
        d = jnp.dot(jnp.concatenate([v_t, ones], axis=0), pb[t * TB:(t + 1) * TB, :],
                    preferred_element_type=F32)
        pv = d if pv is None else pv + d
    l_blk = pv[dv:dv + 1, :]
    if first:
        l_ref[...] = l_blk
        acc_ref[...] = pv[:dv, :]
    else:
        alpha = jnp.exp2(m_ref[...] - m_new)
        l_ref[...] = alpha * l_ref[...] + l_blk
        acc_ref[...] = alpha * acc_ref[...] + pv[:dv, :]
    m_ref[...] = m_new


def _attn_pipeline(score_fn, v_fn, head_scores, head_v, states, slot_a, slot_b, n):
    def stash(j, slot):
        for s, (s_ref, mb_ref) in zip(score_fn(j), slot):
            s_ref[...] = s
            mb_ref[...] = jnp.max(s, axis=0, keepdims=True)

    def update(j, slot):
        for (s_ref, mb_ref), v_parts, state in zip(slot, v_fn(j), states):
            _attn_update(s_ref[...], mb_ref[...], v_parts, state, False)

    stash(0, slot_a)
    for s, v_parts, state in zip(head_scores, head_v, states):
        _attn_update(s, jnp.max(s, axis=0, keepdims=True), v_parts, state, True)

    def pair(i, carry):
        j = 2 * i
        stash(j + 1, slot_b)
        update(j, slot_a)
        stash(j + 2, slot_a)
        update(j + 1, slot_b)
        return carry

    lax.fori_loop(0, n // 2 - 1, pair, 0)
    stash(n - 1, slot_b)
    update(n - 2, slot_a)
    update(n - 1, slot_b)


def _attn_pipeline3(score_fn, v_fn, head_scores, head_v, states, slots, n):
    ones = jnp.ones((SUM_ROWS, TB), BF16)

    def stage_a(j, parity):
        for sc, (s_ref, mb_ref, _, _) in zip(score_fn(j), slots[parity]):
            s_ref[...] = sc
            mb_ref[...] = jnp.max(sc, axis=0, keepdims=True)

    def stage_b(parity):
        for (s_ref, mb_ref, p_ref, alpha_ref), (m_ref, _, _) in zip(slots[parity], states):
            m_new = jnp.maximum(m_ref[...], mb_ref[...])
            alpha_ref[...] = jnp.exp2(m_ref[...] - m_new)
            p_ref[...] = jnp.exp2(s_ref[...] - m_new).astype(BF16)
            m_ref[...] = m_new

    def stage_c(j, parity):
        for (_, _, p_ref, alpha_ref), v_parts, (_, l_ref, acc_ref) in zip(slots[parity], v_fn(j), states):
            dv = acc_ref.shape[0]
            pv = None
            for t, v_t in enumerate(v_parts):
                d = jnp.dot(jnp.concatenate([v_t, ones], axis=0), p_ref[t * TB:(t + 1) * TB, :],
                            preferred_element_type=F32)
                pv = d if pv is None else pv + d
            alpha = alpha_ref[...]
            l_ref[...] = alpha * l_ref[...] + pv[dv:dv + 1, :]
            acc_ref[...] = alpha * acc_ref[...] + pv[:dv, :]

    stage_a(0, 0)
    for sc, v_parts, state in zip(head_scores, head_v, states):
        _attn_update(sc, jnp.max(sc, axis=0, keepdims=True), v_parts, state, True)
    stage_a(1, 1)
    stage_b(0)

    def pair(i, carry):
        j = 2 * i
        stage_a(j + 2, 0)
        stage_b(1)
        stage_c(j, 0)
        stage_a(j + 3, 1)
        stage_b(0)
        stage_c(j + 1, 1)
        return carry

    lax.fori_loop(0, n // 2 - 1, pair, 0)
    stage_b(1)
    stage_c(n - 2, 0)
    stage_c(n - 1, 1)


def _load_q_t(qt_ref, rows):
    return jnp.concatenate([qt_ref[t, rows, :] for t in range(qt_ref.shape[0])], axis=1)


def _lat_chunk_fns(k_ref, vt_ref, v_rows, tk=TK):
    k_fn = lambda j: k_ref[pl.ds(pl.multiple_of(j * tk, tk), tk), :]
    v_fn = lambda j: [[vt_ref[(tk // TB) * j + t, rows, :] for t in range(tk // TB)] for rows in v_rows]
    return k_fn, v_fn


def _slot_scratch3(n_chain, tk, tq):
    return [pltpu.VMEM((tk, tq), F32), pltpu.VMEM((1, tq), F32),
            pltpu.VMEM((tk, tq), BF16), pltpu.VMEM((1, tq), F32)] * (2 * n_chain)


def _split_slots3(refs, n_chain):
    quads = [tuple(refs[4 * i:4 * i + 4]) for i in range(2 * n_chain)]
    return quads[:n_chain], quads[n_chain:]


def _slot_scratch(n_chain, tq):
    return [pltpu.VMEM((TK, tq), F32), pltpu.VMEM((1, tq), F32)] * (2 * n_chain)


def _split_slots(refs, n_chain):
    pairs = [(refs[2 * i], refs[2 * i + 1]) for i in range(2 * n_chain)]
    return pairs[:n_chain], pairs[n_chain:]


def _diff_attn_kernel(*refs, has_lat, lambda_init):
    if has_lat:
        qt_ref, kc_ref, vtc_ref, k_ref, vt_ref, lam_ref, sg_ref, o_ref = refs[:8]
        scratch = refs[8:]
    else:
        qt_ref, kc_ref, vtc_ref, lam_ref, sg_ref, o_ref = refs[:6]
        scratch = refs[6:]
    states = (scratch[0:3], scratch[3:6])
    halves = (slice(0, DIFF_D), slice(DIFF_D, 2 * DIFF_D))

    def scores_of(k):
        return [_scores(k[:, halves[idx]], _load_q_t(qt_ref, halves[idx])) for idx in range(2)]

    ctx_scores = scores_of(kc_ref[...])
    v_rows = [slice(0, DIFF_V)] * 2
    ctx_v = [[vtc_ref[0]]] * 2
    if has_lat:
        k_fn, v_fn = _lat_chunk_fns(k_ref, vt_ref, v_rows)
        slot_a, slot_b = _split_slots(scratch[6:], 2)
        _attn_pipeline(lambda j: scores_of(k_fn(j)), v_fn, ctx_scores, ctx_v, states,
                       slot_a, slot_b, SEQ // TK)
    else:
        for s, v_parts, state in zip(ctx_scores, ctx_v, states):
            _attn_update(s, jnp.max(s, axis=0, keepdims=True), v_parts, state, True)

    lam_v = lam_ref[...]
    lam = (jnp.exp(jnp.sum(lam_v[0:1] * lam_v[1:2], axis=-1, keepdims=True))
           - jnp.exp(jnp.sum(lam_v[2:3] * lam_v[3:4], axis=-1, keepdims=True)) + lambda_init)
    (_, l0, a0), (_, l1, a1) = states
    o_t = a0[...] * (1.0 / l0[...]) - lam * (a1[...] * (1.0 / l1[...]))
    o = o_t.T
    o = o * lax.rsqrt(jnp.mean(o * o, axis=-1, keepdims=True) + EPS) * sg_ref[...]
    o_ref[...] = (o * (1.0 - lambda_init)).astype(BF16)


def _diff_attn(q_t, k, v_t, lam_tab, subln_g, lambda_init, has_lat):
    ctx_blk0 = T_LAT // CTX
    if has_lat:
        tq = TQ
        grid = (BATCH, DIFF_H, SEQ // tq)
        q_blk = lambda b, h, i: b * (SEQ // tq) + i
        o_blk = q_blk
    else:
        tq = CTX
        grid = (BATCH, DIFF_H, 1)
        q_blk = lambda b, h, i: ctx_blk0 + b
        o_blk = lambda b, h, i: b
    in_specs = [
        pl.BlockSpec((tq // TB, DIFF_V, TB), lambda b, h, i: (q_blk(b, h, i), h, 0)),
        pl.BlockSpec((CTX, DIFF_V), lambda b, h, i: (ctx_blk0 + b, h)),
        pl.BlockSpec((1, DIFF_V, TB), lambda b, h, i: (ctx_blk0 + b, h, 0)),
    ]
    args = [q_t, k, v_t]
    if has_lat:
        in_specs += [
            pl.BlockSpec((SEQ, DIFF_V), lambda b, h, i: (b, h)),
            pl.BlockSpec((SEQ // TB, DIFF_V, TB), lambda b, h, i: (b, h, 0)),
        ]
        args += [k, v_t]
    in_specs += [
        pl.BlockSpec((8, DIFF_D), lambda b, h, i: (0, 0)),
        pl.BlockSpec((1, DIFF_V), lambda b, h, i: (0, 0)),
    ]
    args += [lam_tab, subln_g.reshape(1, DIFF_V)]
    return pl.pallas_call(
        functools.partial(_diff_attn_kernel, has_lat=has_lat, lambda_init=lambda_init),
        grid=grid,
        in_specs=in_specs,
        out_specs=pl.BlockSpec((tq, DIFF_V), lambda b, h, i: (o_blk(b, h, i), h)),
        out_shape=jax.ShapeDtypeStruct((T_LAT if has_lat else T_CTX, DIFF_H * DIFF_V), BF16),
        scratch_shapes=([pltpu.VMEM((1, tq), F32), pltpu.VMEM((1, tq), F32), pltpu.VMEM((DIFF_V, tq), F32)] * 2
                        + (_slot_scratch(2, tq) if has_lat else [])),
        compiler_params=_cparams(("arbitrary",) * 3),
        name="diff_attn_lat" if has_lat else "diff_attn_ctx",
    )(*args)


MLA_HPS = 2
MLA_TQ = 1024
MLA_TK = 512


def _mla_attn_kernel(qt_ref, kc_ref, vtc_ref, k_ref, vt_ref, o_ref, *scratch):
    states = [scratch[3 * i:3 * i + 3] for i in range(MLA_HPS)]
    qk = [slice(i * MLA_QK_PAD, (i + 1) * MLA_QK_PAD) for i in range(MLA_HPS)]
    v_rows = [slice(i * MLA_V, (i + 1) * MLA_V) for i in range(MLA_HPS)]

    def scores_of(k):
        return [_scores(k[:, qk[i]], _load_q_t(qt_ref, qk[i])) for i in range(MLA_HPS)]

    k_fn, v_fn = _lat_chunk_fns(k_ref, vt_ref, v_rows, MLA_TK)
    _attn_pipeline3(lambda j: scores_of(k_fn(j)), v_fn, scores_of(kc_ref[...]),
                    [[vtc_ref[0, rows, :]] for rows in v_rows], states,
                    _split_slots3(scratch[3 * MLA_HPS:], MLA_HPS), SEQ // MLA_TK)
    for i, (_, l_ref, acc_ref) in enumerate(states):
        o_ref[:, v_rows[i]] = (acc_ref[...] * (1.0 / l_ref[...])).T.astype(BF16)


def _mla_attn(q_t, k, v_t):
    ctx_blk0 = T_LAT // CTX
    qk_w, v_w = MLA_HPS * MLA_QK_PAD, MLA_HPS * MLA_V
    return pl.pallas_call(
        _mla_attn_kernel,
        grid=(BATCH, MLA_H // MLA_HPS, SEQ // MLA_TQ),
        in_specs=[
            pl.BlockSpec((MLA_TQ // TB, qk_w, TB), lambda b, h, i: (b * (SEQ // MLA_TQ) + i, h, 0)),
            pl.BlockSpec((CTX, qk_w), lambda b, h, i: (ctx_blk0 + b, h)),
            pl.BlockSpec((1, v_w, TB), lambda b, h, i: (ctx_blk0 + b, h, 0)),
            pl.BlockSpec((SEQ, qk_w), lambda b, h, i: (b, h)),
            pl.BlockSpec((SEQ // TB, v_w, TB), lambda b, h, i: (b, h, 0)),
        ],
        out_specs=pl.BlockSpec((MLA_TQ, v_w), lambda b, h, i: (b * (SEQ // MLA_TQ) + i, h)),
        out_shape=jax.ShapeDtypeStruct((T_LAT, MLA_H * MLA_V), BF16),
        scratch_shapes=([pltpu.VMEM((1, MLA_TQ), F32), pltpu.VMEM((1, MLA_TQ), F32), pltpu.VMEM((MLA_V, MLA_TQ), F32)]
                        * MLA_HPS + _slot_scratch3(MLA_HPS, MLA_TK, MLA_TQ)),
        compiler_params=_cparams(("arbitrary",) * 3),
        name="mla_attn",
    )(q_t, k, v_t, k, v_t)


def _mla_proj_kernel(x_ref, g_ref, sh_ref, sc_ref, wd_ref, qg_ref, kvg_ref, wuq_ref, wukv_ref,
                     cos_ref, sa_ref, sb_ref, qt_ref, k_ref, vt_ref):
    h = _norm_mod(x_ref[...], g_ref[...], sh_ref[...], sc_ref[...]).astype(BF16)
    c = jnp.dot(h, wd_ref[...], preferred_element_type=F32)
    cq = c[:, :MLA_QR_PAD]
    ckv = c[:, MLA_QR_PAD:MLA_QR_PAD + MLA_KVR]
    kr = c[:, MLA_QR_PAD + MLA_KVR:]
    cqn = cq * lax.rsqrt(jnp.sum(cq * cq, axis=-1, keepdims=True) * (1.0 / MLA_QR) + EPS) * qg_ref[...]
    ckvn = ckv * lax.rsqrt(jnp.mean(ckv * ckv, axis=-1, keepdims=True) + EPS) * kvg_ref[...]
    cos, sa, sb = cos_ref[...], sa_ref[...], sb_ref[...]
    kr_rot = _rope(kr, cos, sa, sb, MLA_ROPE_CH).astype(BF16)
    q = jnp.dot(cqn.astype(BF16), wuq_ref[...], preferred_element_type=F32) * MLA_Q_SCALE
    kv = jnp.dot(ckvn.astype(BF16), wukv_ref[...], preferred_element_type=F32)
    for hd in range(MLA_H):
        base = hd * MLA_QK_PAD
        nope, rope = slice(base, base + MLA_NOPE), slice(base + MLA_NOPE, base + MLA_QK_PAD)
        qt_ref[0, nope, :] = q[:, nope].T.astype(BF16)
        qt_ref[0, rope, :] = _rope(q[:, rope], cos, sa, sb, MLA_ROPE_CH).T.astype(BF16)
        k_ref[:, nope] = kv[:, hd * MLA_NOPE:(hd + 1) * MLA_NOPE].astype(BF16)
        k_ref[:, rope] = kr_rot
        vcols = slice(MLA_H * MLA_NOPE + hd * MLA_V, MLA_H * MLA_NOPE + (hd + 1) * MLA_V)
        vt_ref[0, hd * MLA_V:(hd + 1) * MLA_V, :] = kv[:, vcols].T.astype(BF16)


def _mla_proj(xs, mod, layer, norm_g, w_d, q_g, kv_g, w_uq, w_ukv, tabs):
    tm = TB
    scale = TM // tm
    const2 = lambda i: (0, 0)
    mod_spec = lambda k: pl.BlockSpec((None, None, 1, D), lambda i: (layer, _mod_row(i // scale), 0, k))
    tab_spec = pl.BlockSpec((tm, LANES), lambda i: (
        jnp.where(i < N_LAT_TILES * scale, i % (TILES_PER_BATCH * scale), TILES_PER_BATCH * scale), 0))
    n_q = MLA_H * MLA_QK_PAD
    return pl.pallas_call(
        _mla_proj_kernel,
        grid=(N_ALL_TILES * scale,),
        in_specs=[
            pl.BlockSpec((tm, D), lambda i: (i, 0)),
            pl.BlockSpec((1, D), const2),
            mod_spec(0), mod_spec(1),
            pl.BlockSpec((D, MLA_C_PAD), const2),
            pl.BlockSpec((1, MLA_QR_PAD), const2),
            pl.BlockSpec((1, MLA_KVR), const2),
            pl.BlockSpec((MLA_QR_PAD, n_q), const2),
            pl.BlockSpec((MLA_KVR, MLA_H * (MLA_NOPE + MLA_V)), const2),
            tab_spec, tab_spec, tab_spec,
        ],
        out_specs=[
            pl.BlockSpec((1, n_q, tm), lambda i: (i, 0, 0)),
            pl.BlockSpec((tm, n_q), lambda i: (i, 0)),
            pl.BlockSpec((1, MLA_H * MLA_V, tm), lambda i: (i, 0, 0)),
        ],
        out_shape=[
            jax.ShapeDtypeStruct((T_ALL // tm, n_q, tm), BF16),
            jax.ShapeDtypeStruct((T_ALL, n_q), BF16),
            jax.ShapeDtypeStruct((T_ALL // tm, MLA_H * MLA_V, tm), BF16),
        ],
        compiler_params=_cparams(("arbitrary",)),
        name=f"mla_proj_l{layer}",
    )(xs, norm_g.reshape(1, D), mod, mod, w_d, q_g, kv_g, w_uq, w_ukv, *tabs)


def _rope_tables(rot_dim):
    a = rot_dim // 2
    ch = a // 2
    n_rows = SEQ // GRID_W
    inv = ROPE_BASE ** (-jnp.arange(0, a, 2, dtype=F32) / a)

    def axis_angles(n):
        ang = jnp.arange(n, dtype=jnp.int32).astype(F32)[:, None] * inv[None, :]
        return jnp.concatenate([ang, ang], axis=-1)

    def grid(fn):
        by_row = jnp.broadcast_to(fn(axis_angles(n_rows))[:, None, :], (n_rows, GRID_W, a))
        by_col = jnp.broadcast_to(fn(axis_angles(GRID_W))[None, :, :], (n_rows, GRID_W, a))
        return jnp.concatenate([by_row, by_col], axis=-1).reshape(SEQ, rot_dim)

    cos, sin = grid(jnp.cos), grid(jnp.sin)
    lane = jnp.arange(rot_dim)
    first = (lane % a) < ch
    sin_a = jnp.where(first[None, :], -sin, 0.0)
    sin_b = jnp.where(first[None, :], 0.0, sin)
    pad = LANES - rot_dim
    cos = jnp.pad(cos, ((0, TM), (0, pad)), constant_values=1.0)
    cos = cos.at[SEQ:, :].set(1.0)
    sin_a = jnp.pad(sin_a, ((0, TM), (0, pad)))
    sin_b = jnp.pad(sin_b, ((0, TM), (0, pad)))
    return cos, sin_a, sin_b


def kernel(x, c, ctx, c_ctx, w_mod, b_mod, norm1_g, norm2_g, w_ff1, w_ff2, a_w_in, a_b_in, a_ln_g, a_w_s, a_b_s, a_w_out, b_w_qkv, b_lam_q1, b_lam_k1, b_lam_q2, b_lam_k2, b_subln_g, b_w_o, c_w_dqkv, c_q_norm_g, c_w_uq, c_kv_norm_g, c_w_ukv, c_w_o, final_g):
    xs = (x.reshape(T_LAT, D), ctx.reshape(T_CTX, D))
    cc = jnp.concatenate([c, c_ctx[None, :], jnp.zeros((8 - BATCH - 1, D), F32)], axis=0)
    mod = _modulation(cc, w_mod, b_mod).reshape(DEPTH, 8, 1, N_MOD * D)

    tabs_b = _rope_tables(DIFF_D)
    tabs_c = _rope_tables(MLA_ROPE)
    w1_b, w2_b = w_ff1[0].astype(BF16), w_ff2[0].astype(BF16)

    out = None
    for l in range(DEPTH):
        kind, idx = l % 3, l // 3
        ctx_later = any(j % 3 != 0 for j in range(l + 1, DEPTH))
        last = l == DEPTH - 1
        n_tiles = N_ALL_TILES if ctx_later else N_LAT_TILES
        if kind == 0:
            b_s = jnp.broadcast_to(a_b_s[idx][:, :, None], (GROUPS, CHUNK, CHUNK))
            t = _gmlp_mix(xs, mod, l, n_tiles, norm1_g[l], a_w_in[idx].astype(BF16), a_b_in[idx],
                          a_ln_g[idx], a_w_s[idx].astype(BF16), b_s)
            xs, h2 = _out_proj(xs, t, a_w_out[idx].astype(BF16), mod, l, n_tiles, norm2_g[l])
        elif kind == 1:
            lambda_init = 0.8 - 0.6 * math.exp(-0.3 * l)
            q_t, k, v_t = _diff_proj(xs, mod, l, norm1_g[l], b_w_qkv[idx].astype(BF16), tabs_b)
            lam_tab = jnp.concatenate([b_lam_q1[idx][None], b_lam_k1[idx][None], b_lam_q2[idx][None],
                                       b_lam_k2[idx][None], jnp.zeros((4, DIFF_D), F32)], axis=0)
            o = _diff_attn(q_t, k, v_t, lam_tab, b_subln_g[idx], lambda_init, True)
            if ctx_later:
                o = (o, _diff_attn(q_t, k, v_t, lam_tab, b_subln_g[idx], lambda_init, False))
            xs, h2 = _out_proj(xs, o, b_w_o[idx].astype(BF16), mod, l, n_tiles, norm2_g[l])
        else:
            w_d = c_w_dqkv[idx]
            w_d = jnp.concatenate([
                jnp.pad(w_d[:, :MLA_QR], ((0, 0), (0, MLA_QR_PAD - MLA_QR))),
                w_d[:, MLA_QR:MLA_QR + MLA_KVR],
                jnp.pad(w_d[:, MLA_QR + MLA_KVR:], ((0, 0), (0, LANES - MLA_ROPE)))], axis=1).astype(BF16)
            w_uq = jnp.pad(c_w_uq[idx].reshape(MLA_QR, MLA_H, MLA_QK),
                           ((0, MLA_QR_PAD - MLA_QR), (0, 0), (0, MLA_QK_PAD - MLA_QK)))
            w_uq = w_uq.reshape(MLA_QR_PAD, MLA_H * MLA_QK_PAD).astype(BF16)
            w_ukv = c_w_ukv[idx].reshape(MLA_KVR, MLA_H, MLA_NOPE + MLA_V)
            w_ukv = jnp.concatenate([w_ukv[:, :, :MLA_NOPE].reshape(MLA_KVR, MLA_H * MLA_NOPE),
                                     w_ukv[:, :, MLA_NOPE:].reshape(MLA_KVR, MLA_H * MLA_V)],
                                    axis=1).astype(BF16)
            q_g = jnp.pad(c_q_norm_g[idx], (0, MLA_QR_PAD - MLA_QR)).reshape(1, MLA_QR_PAD)
            q_t, k, v_t = _mla_proj(xs, mod, l, norm1_g[l], w_d, q_g, c_kv_norm_g[idx].reshape(1, MLA_KVR),
                                    w_uq, w_ukv, tabs_c)
            o = _mla_attn(q_t, k, v_t)
            xs, h2 = _out_proj(xs, o, c_w_o[idx].astype(BF16), mod, l, N_LAT_TILES, norm2_g[l])
        if ctx_later and kind == 2:
            raise NotImplementedError("context update after an MLA layer is not needed at this depth")
        res = _ffn(xs, h2, mod, l, n_tiles, w1_b, w2_b, final_g, last, None if last else (w_ff1, w_ff2))
        if last:
            out, = res
        else:
            xs, w1_b, w2_b = res
    return out.reshape(BATCH, SEQ, D)
```
